```python
import numpy as np
import jax
import jax.numpy as jnp
from jax import lax


D_MODEL = 1024
BATCH = 8
SEQ = 4096
DEPTH = 4

EPS = 1e-6
ROPE_THETA = 10000.0
NEG = -1e30
TINY = 1e-30
Q_BLOCK = 128

NSA_HEADS = 8
NSA_KV_GROUPS = 2
NSA_HEAD_DIM = 64
CMP_BLOCK = 32
CMP_STRIDE = 16
CMP_HIDDEN = 4 * NSA_HEAD_DIM
SLC_BLOCK = 64
SLC_TOPK = 16
N_LOCAL_FORCED = 2
FORCE_BONUS = 1e4
SLC_Q_BLOCK = 64
WINDOW = 512

MLA_HEADS = 8
MLA_Q_LORA = 256
MLA_KV_LORA = 128
MLA_NOPE = 64
MLA_ROPE = 32
MLA_V = 64

D_FF = 4 * D_MODEL

NSA_Q_COLS = NSA_HEADS * NSA_HEAD_DIM
NSA_KV_COLS = 2 * NSA_KV_GROUPS * NSA_HEAD_DIM
NSA_GATE_COLS = 3 * NSA_HEADS
IN_SPLITS = (NSA_Q_COLS, NSA_KV_COLS, NSA_KV_COLS, NSA_KV_COLS, NSA_GATE_COLS,
             MLA_Q_LORA, MLA_KV_LORA, MLA_ROPE, D_MODEL, D_MODEL)
IN_COLS = sum(IN_SPLITS)

kernel_name = 'hybrid_nsa_mla_gated_block'


def rms_norm(x, g):
    xf = x.astype(jnp.float32)
    y = xf * lax.rsqrt(jnp.mean(xf * xf, axis=-1, keepdims=True) + EPS)
    return (y * g.astype(jnp.float32)).astype(x.dtype)


def rope(x, pos):
    d = x.shape[-1]
    inv = ROPE_THETA ** (-jnp.arange(0, d, 2, dtype=jnp.float32) / d)
    ang = pos.astype(jnp.float32)[:, None] * inv[None, :]
    cos, sin = jnp.cos(ang), jnp.sin(ang)
    x1, x2 = jnp.split(x.astype(jnp.float32), 2, axis=-1)
    out = jnp.concatenate([x1 * cos - x2 * sin, x1 * sin + x2 * cos], axis=-1)
    return out.astype(x.dtype)


def masked_softmax(s, mask):
    sf = jnp.where(mask, s.astype(jnp.float32), NEG)
    m = jnp.max(sf, axis=-1, keepdims=True)
    e = jnp.where(mask, jnp.exp(sf - m), 0.0)
    p = e / jnp.maximum(jnp.sum(e, axis=-1, keepdims=True), TINY)
    return p.astype(s.dtype)


def split_cols(t):
    offsets = np.cumsum(np.array(IN_SPLITS))[:-1].tolist()
    return jnp.split(t, offsets, axis=-1)


def overlap_matrix(n_cmp, n_slc):
    start = np.arange(n_cmp) * CMP_STRIDE
    sel = np.arange(n_slc) * SLC_BLOCK
    lo = np.maximum(start[:, None], sel[None, :])
    hi = np.minimum(start[:, None] + CMP_BLOCK, sel[None, :] + SLC_BLOCK)
    return (np.clip(hi - lo, 0, None) / CMP_BLOCK).astype(np.float32)


_gather_bg = jax.vmap(jax.vmap(lambda t, i: t[i]))


def nsa_attention(q_flat, kv_cmp, kv_slc, kv_win, gate_logits, cmp_pe, cmp_w1, cmp_w2, pos):
    B, S, _ = q_flat.shape
    G, R, hd = NSA_KV_GROUPS, NSA_HEADS // NSA_KV_GROUPS, NSA_HEAD_DIM
    scale = hd ** -0.5
    q = rope(q_flat.reshape(B, S, G, R, hd).transpose(0, 2, 3, 1, 4), pos)

    def split_kv(kv):
        a = kv.reshape(B, S, 2, G, hd).transpose(2, 0, 3, 1, 4)
        return rope(a[0], pos), a[1]

    k_c, v_c = split_kv(kv_cmp)
    k_s, v_s = split_kv(kv_slc)
    k_w, v_w = split_kv(kv_win)

    n_cmp = (S - CMP_BLOCK) // CMP_STRIDE + 1
    blk_idx = jnp.arange(n_cmp)[:, None] * CMP_STRIDE + jnp.arange(CMP_BLOCK)[None, :]

    def compress(t, j):
        tb = t[:, :, blk_idx] + cmp_pe[j]
        h = jax.nn.silu(tb.reshape(B, G, n_cmp, CMP_BLOCK * hd) @ cmp_w1[j])
        return h @ cmp_w2[j]

    kc = compress(k_c, 0)
    vc = compress(v_c, 1)
    cmp_end = jnp.arange(n_cmp) * CMP_STRIDE + CMP_BLOCK - 1
    cmp_mask = cmp_end[None, :] <= pos[:, None]
    s_cmp = jnp.einsum('bgrsd,bgnd->bgrsn', q, kc) * scale
    p_cmp = masked_softmax(s_cmp, cmp_mask)
    o_cmp = jnp.einsum('bgrsn,bgnd->bgrsd', p_cmp, vc)

    n_slc = S // SLC_BLOCK
    k_top = min(SLC_TOPK, n_slc)
    ov = jnp.asarray(overlap_matrix(n_cmp, n_slc))
    imp = jnp.einsum('bgrsn,nj->bgsj', p_cmp.astype(jnp.float32), ov)
    blk = jnp.arange(n_slc)[None, :]
    cur = (pos // SLC_BLOCK)[:, None]
    causal = blk <= cur
    forced = (blk == 0) | ((cur - blk >= 0) & (cur - blk < N_LOCAL_FORCED))
    score = jnp.where(causal, imp + jnp.where(forced, FORCE_BONUS, 0.0), NEG)
    vals, sel_idx = lax.top_k(score, k_top)
    sel_ok = vals > 0.5 * NEG

    QS = SLC_Q_BLOCK
    nq = S // QS
    qc = jnp.moveaxis(q.reshape(B, G, R, nq, QS, hd), 3, 0)
    ic = jnp.moveaxis(sel_idx.reshape(B, G, nq, QS, k_top), 2, 0)
    okc = jnp.moveaxis(sel_ok.reshape(B, G, nq, QS, k_top), 2, 0)
    pc = pos.reshape(nq, QS)

    def slc_block(args):
        qb, ib, okb, tb = args
        tok = (ib[..., None] * SLC_BLOCK + jnp.arange(SLC_BLOCK)).reshape(
            ib.shape[:-1] + (ib.shape[-1] * SLC_BLOCK,))
        kg = _gather_bg(k_s, tok)
        vg = _gather_bg(v_s, tok)
        mask = jnp.repeat(okb, SLC_BLOCK, axis=-1) & (tok <= tb[:, None])
        s = jnp.einsum('bgrqd,bgqtd->bgrqt', qb, kg) * scale
        p = masked_softmax(s, mask[:, :, None])
        return jnp.einsum('bgrqt,bgqtd->bgrqd', p, vg)

    o_slc = lax.map(slc_block, (qc, ic, okc, pc))
    o_slc = jnp.moveaxis(o_slc, 0, 3).reshape(B, G, R, S, hd)

    nb = S // Q_BLOCK
    pad = ((0, 0), (0, 0), (WINDOW, 0), (0, 0))
    band = jnp.arange(nb)[:, None] * Q_BLOCK + jnp.arange(Q_BLOCK + WINDOW)[None, :]
    kband = jnp.moveaxis(jnp.pad(k_w, pad)[:, :, band], 2, 0)
    vband = jnp.moveaxis(jnp.pad(v_w, pad)[:, :, band], 2, 0)
    kpos = band - WINDOW
    qw = jnp.moveaxis(q.reshape(B, G, R, nb, Q_BLOCK, hd), 3, 0)
    pw = pos.reshape(nb, Q_BLOCK)

    def win_block(args):
        qb, kb, vb, tq, tk = args
        diff = tq[:, None] - tk[None, :]
        mask = (tk[None, :] >= 0) & (diff >= 0) & (diff < WINDOW)
        s = jnp.einsum('bgrqd,bgkd->bgrqk', qb, kb) * scale
        p = masked_softmax(s, mask)
        return jnp.einsum('bgrqk,bgkd->bgrqd', p, vb)

    o_win = lax.map(win_block, (qw, kband, vband, pw, kpos))
    o_win = jnp.moveaxis(o_win, 0, 3).reshape(B, G, R, S, hd)

    g = jax.nn.sigmoid(gate_logits).reshape(B, S, 3, G, R).transpose(2, 0, 3, 4, 1)[..., None]
    o = g[0] * o_cmp + g[1] * o_slc + g[2] * o_win
    return o.transpose(0, 3, 1, 2, 4).reshape(B, S, NSA_HEADS * hd)


def mla_attention(cq, ckv, k_rope_raw, q_norm, kv_norm, w_uq, w_ukv, pos):
    B, S, _ = cq.shape
    H = MLA_HEADS
    dq = MLA_NOPE + MLA_ROPE
    q = (rms_norm(cq, q_norm) @ w_uq).reshape(B, S, H, dq).transpose(0, 2, 1, 3)
    q = jnp.concatenate([q[..., :MLA_NOPE], rope(q[..., MLA_NOPE:], pos)], axis=-1)
    kv = (rms_norm(ckv, kv_norm) @ w_ukv).reshape(B, S, H, MLA_NOPE + MLA_V).transpose(0, 2, 1, 3)
    k_nope, v = kv[..., :MLA_NOPE], kv[..., MLA_NOPE:]
    k_pe = rope(k_rope_raw[:, None], pos)
    k = jnp.concatenate([k_nope, jnp.broadcast_to(k_pe, (B, H, S, MLA_ROPE))], axis=-1)
    scale = dq ** -0.5
    nb = S // Q_BLOCK
    qc = jnp.moveaxis(q.reshape(B, H, nb, Q_BLOCK, dq), 2, 0)
    pc = pos.reshape(nb, Q_BLOCK)

    def blk(args):
        qb, tq = args
        s = jnp.einsum('bhqd,bhkd->bhqk', qb, k) * scale
        p = masked_softmax(s, tq[:, None] >= pos[None, :])
        return jnp.einsum('bhqk,bhkd->bhqd', p, v)

    o = jnp.moveaxis(lax.map(blk, (qc, pc)), 0, 2).reshape(B, H, S, MLA_V)
    return o.transpose(0, 2, 1, 3).reshape(B, S, H * MLA_V)


def setup_inputs(seed: int = 0) -> dict:
    key = jax.random.key(seed)
    ks = jax.random.split(key, 20)
    L = DEPTH
    hd = NSA_HEAD_DIM

    def nrm(k, shape, scale):
        return jax.random.normal(k, shape, jnp.float32) * scale

    def gain(k, shape):
        return 1.0 + nrm(k, shape, 0.02)

    return {
        'x': nrm(ks[0], (BATCH, SEQ, D_MODEL), 1.0),
        'attn_norm': gain(ks[1], (L, D_MODEL)),
        'w_in': nrm(ks[2], (L, D_MODEL, IN_COLS), D_MODEL ** -0.5),
        'cmp_pe': nrm(ks[3], (L, 2, CMP_BLOCK, hd), 0.1),
        'cmp_w1': nrm(ks[4], (L, 2, CMP_BLOCK * hd, CMP_HIDDEN), (CMP_BLOCK * hd) ** -0.5),
        'cmp_w2': nrm(ks[5], (L, 2, CMP_HIDDEN, hd), CMP_HIDDEN ** -0.5),
        'nsa_w_o': nrm(ks[6], (L, NSA_Q_COLS, D_MODEL), NSA_Q_COLS ** -0.5),
        'mla_q_norm': gain(ks[7], (L, MLA_Q_LORA)),
        'mla_kv_norm': gain(ks[8], (L, MLA_KV_LORA)),
        'mla_w_uq': nrm(ks[9], (L, MLA_Q_LORA, MLA_HEADS * (MLA_NOPE + MLA_ROPE)), MLA_Q_LORA ** -0.5),
        'mla_w_ukv': nrm(ks[10], (L, MLA_KV_LORA, MLA_HEADS * (MLA_NOPE + MLA_V)), MLA_KV_LORA ** -0.5),
        'mla_w_o': nrm(ks[11], (L, MLA_HEADS * MLA_V, D_MODEL), (MLA_HEADS * MLA_V) ** -0.5),
        'w_out': nrm(ks[12], (L, D_MODEL, D_MODEL), D_MODEL ** -0.5),
        'mlp_norm': gain(ks[13], (L, D_MODEL)),
        'w_up': nrm(ks[14], (L, D_MODEL, D_FF), D_MODEL ** -0.5),
        'w_down': nrm(ks[15], (L, D_FF, D_MODEL), D_FF ** -0.5),
        'final_norm': gain(ks[16], (D_MODEL,)),
    }


def reference(x, attn_norm, w_in, cmp_pe, cmp_w1, cmp_w2, nsa_w_o, mla_q_norm, mla_kv_norm,
              mla_w_uq, mla_w_ukv, mla_w_o, w_out, mlp_norm, w_up, w_down, final_norm):
    S = x.shape[1]
    pos = jnp.arange(S, dtype=jnp.int32)
    for l in range(DEPTH):
        xn = rms_norm(x, attn_norm[l])
        (nsa_q, kv_c, kv_s, kv_w, nsa_g, cq, ckv, k_rope_raw,
         gate_a, gate_b) = split_cols(xn @ w_in[l])
        o_a = nsa_attention(nsa_q, kv_c, kv_s, kv_w, nsa_g, cmp_pe[l], cmp_w1[l], cmp_w2[l], pos)
        o_b = mla_attention(cq, ckv, k_rope_raw, mla_q_norm[l], mla_kv_norm[l],
                            mla_w_uq[l], mla_w_ukv[l], pos)
        merged = (jax.nn.sigmoid(gate_a) * (o_a @ nsa_w_o[l])
                  + jax.nn.sigmoid(gate_b) * (o_b @ mla_w_o[l]))
        x = x + merged @ w_out[l]
        hn = rms_norm(x, mlp_norm[l])
        x = x + jnp.square(jax.nn.relu(hn @ w_up[l])) @ w_down[l]
    return rms_norm(x, final_norm)
```

```python
import functools

import numpy as np
import jax
import jax.numpy as jnp
from jax import lax
from jax.experimental import pallas as pl
from jax.experimental.pallas import tpu as pltpu

EPS = 1e-6
ROPE_THETA = 10000.0
NEG = -1e30
TINY = 1e-30

D_MODEL = 1024
NSA_HEADS = 8
NSA_KV_GROUPS = 2
NSA_REP = NSA_HEADS // NSA_KV_GROUPS
NSA_HEAD_DIM = 64
CMP_BLOCK = 32
CMP_STRIDE = 16
CMP_HIDDEN = 4 * NSA_HEAD_DIM
SLC_BLOCK = 64
SLC_TOPK = 16
N_LOCAL_FORCED = 2
FORCE_BONUS = 1e4
WINDOW = 512

MLA_HEADS = 8
MLA_Q_LORA = 256
MLA_KV_LORA = 128
MLA_NOPE = 64
MLA_ROPE = 32
MLA_V = 64
D_FF = 4 * D_MODEL

LANES = 128
SUBLANES = 8
VMEM_LIMIT_BYTES = 56 * 1024 * 1024

TOKEN_TILE = 512
ATTN_TILE = 256

_OFF_Q = 0
_OFF_KVC = _OFF_Q + NSA_HEADS * NSA_HEAD_DIM
_OFF_KVS = _OFF_KVC + 2 * NSA_KV_GROUPS * NSA_HEAD_DIM
_OFF_KVW = _OFF_KVS + 2 * NSA_KV_GROUPS * NSA_HEAD_DIM
_OFF_NG = _OFF_KVW + 2 * NSA_KV_GROUPS * NSA_HEAD_DIM
_OFF_CQ = _OFF_NG + 3 * NSA_HEADS
_OFF_CKV = _OFF_CQ + MLA_Q_LORA
_OFF_KR = _OFF_CKV + MLA_KV_LORA
_OFF_GA = _OFF_KR + MLA_ROPE
_OFF_GB = _OFF_GA + D_MODEL

_R_Q = 0
_R_CMP = _R_Q + NSA_HEADS * LANES
_R_VS = _R_CMP + 256
_R_VW = _R_VS + NSA_KV_GROUPS * LANES
_R_NG = _R_VW + NSA_KV_GROUPS * LANES
_R_CQ = _R_NG + 3 * NSA_KV_GROUPS * LANES
_R_CKV = _R_CQ + MLA_Q_LORA
_R_GA = _R_CKV + MLA_KV_LORA
_R_GB = _R_GA + D_MODEL
_R_END = _R_GB + D_MODEL


def _nt_dot(a, b):
    return lax.dot_general(a, b, (((1,), (1,)), ((), ())), preferred_element_type=jnp.float32)


def _dot(a, b):
    return jnp.dot(a, b, preferred_element_type=jnp.float32)


def _rms(xf, g):
    return xf * lax.rsqrt(jnp.mean(xf * xf, axis=-1, keepdims=True) + EPS) * g


def _rope_rows(y, c, s1, s2, sh):
    return y * c + pltpu.roll(y, sh, 1) * s1 + pltpu.roll(y, LANES - sh, 1) * s2


def _rope_cols(y, cos_t, sin_t):
    h = y.shape[0] // 2
    a, b = y[:h], y[h:]
    return jnp.concatenate([a * cos_t - b * sin_t, a * sin_t + b * cos_t], axis=0)


def _in_proj_kernel(x_ref, g_ref, w_ref, wt_ref, qng_ref, kvng_ref, wuq_ref, wv_ref, wkt_ref,
                    cq_ref, s1q_ref, s2q_ref, cc_ref, s1c_ref, s2c_ref, cost_ref, sint_ref,
                    cm_ref, s1m_ref, s2m_ref, cost16_ref, sint16_ref,
                    qn_ref, cmp_ref, vs_ref, vw_ref, ng_ref, ga_ref, gb_ref, kts_ref, ktw_ref,
                    qm_ref, vm_ref, ktm_ref, *, tm, tk, s_tiles):
    si = pl.program_id(0) % s_tiles
    hd = NSA_HEAD_DIM
    xn = _rms(x_ref[...], g_ref[...]).astype(jnp.bfloat16)

    lane = lax.broadcasted_iota(jnp.int32, (tm, LANES), 1)
    ones_col = jnp.where(lane == hd, 1.0, 0.0)

    y = _dot(xn, w_ref[:, _R_Q:_R_CMP])
    cq, s1q, s2q = cq_ref[...], s1q_ref[...], s2q_ref[...]
    for h in range(NSA_HEADS):
        blk = y[:, h * LANES:(h + 1) * LANES]
        qn_ref[:, h * LANES:(h + 1) * LANES] = _rope_rows(blk, cq, s1q, s2q, hd // 2).astype(jnp.bfloat16)

    y = _dot(xn, w_ref[:, _R_CMP:_R_VS])
    cmp_ref[:, :LANES] = _rope_rows(y[:, :LANES], cc_ref[...], s1c_ref[...], s2c_ref[...], hd // 2)
    cmp_ref[:, LANES:] = y[:, LANES:]

    y = _dot(xn, w_ref[:, _R_VS:_R_VW])
    for g in range(NSA_KV_GROUPS):
        vs_ref[:, g * LANES:(g + 1) * LANES] = (y[:, g * LANES:(g + 1) * LANES] + ones_col).astype(jnp.bfloat16)
    y = _dot(xn, w_ref[:, _R_VW:_R_NG])
    for g in range(NSA_KV_GROUPS):
        vw_ref[:, g * LANES:(g + 1) * LANES] = (y[:, g * LANES:(g + 1) * LANES] + ones_col).astype(jnp.bfloat16)

    ng_ref[...] = jax.nn.sigmoid(_dot(xn, w_ref[:, _R_NG:_R_CQ]))
    ga_ref[...] = jax.nn.sigmoid(_dot(xn, w_ref[:, _R_GA:_R_GB])).astype(jnp.bfloat16)
    gb_ref[...] = jax.nn.sigmoid(_dot(xn, w_ref[:, _R_GB:_R_END])).astype(jnp.bfloat16)

    cqn = _rms(_dot(xn, w_ref[:, _R_CQ:_R_CKV]), qng_ref[...]).astype(jnp.bfloat16)
    y = _dot(cqn, wuq_ref[...])
    cm, s1m, s2m = cm_ref[...], s1m_ref[...], s2m_ref[...]
    for h in range(MLA_HEADS):
        blk = y[:, h * LANES:(h + 1) * LANES]
        qm_ref[:, h * LANES:(h + 1) * LANES] = _rope_rows(blk, cm, s1m, s2m, MLA_ROPE // 2).astype(jnp.bfloat16)

    ckvn = _rms(_dot(xn, w_ref[:, _R_CKV:_R_GA]), kvng_ref[...]).astype(jnp.bfloat16)
    y = _dot(ckvn, wv_ref[...])
    for h in range(MLA_HEADS):
        vm_ref[:, h * LANES:(h + 1) * LANES] = (y[:, h * LANES:(h + 1) * LANES] + ones_col).astype(jnp.bfloat16)
    knt = _nt_dot(wkt_ref[...], ckvn)

    yt = _nt_dot(wt_ref[...], xn)
    cos_t, sin_t = cost_ref[...], sint_ref[...]
    kpos = si * tm + lax.broadcasted_iota(jnp.int32, (SLC_BLOCK, tm), 1)
    blk_id = lax.broadcasted_iota(jnp.int32, (SLC_BLOCK, tm), 0)
    sel_rows = jnp.where((kpos // SLC_BLOCK) == blk_id, 1.0, 0.0)
    zero_rows = jnp.zeros((LANES - hd, tm), jnp.float32)
    for g in range(NSA_KV_GROUPS):
        ks = _rope_cols(yt[g * hd:(g + 1) * hd], cos_t, sin_t)
        kw = _rope_cols(yt[(NSA_KV_GROUPS + g) * hd:(NSA_KV_GROUPS + g + 1) * hd], cos_t, sin_t)
        ks = jnp.concatenate([ks, sel_rows], axis=0).astype(jnp.bfloat16)
        kw = jnp.concatenate([kw, zero_rows], axis=0).astype(jnp.bfloat16)
        for c in range(tm // tk):
            kts_ref[0, g, c] = ks[:, c * tk:(c + 1) * tk]
            ktw_ref[0, g, c] = kw[:, c * tk:(c + 1) * tk]
    kpe = _rope_cols(yt[2 * NSA_KV_GROUPS * hd:], cost16_ref[...], sint16_ref[...])
    pad_rows = jnp.zeros((LANES - MLA_NOPE - MLA_ROPE, tm), jnp.float32)
    for h in range(MLA_HEADS):
        km = jnp.concatenate([knt[h * MLA_NOPE:(h + 1) * MLA_NOPE], kpe, pad_rows], axis=0).astype(jnp.bfloat16)
        for c in range(tm // tk):
            ktm_ref[0, h, c] = km[:, c * tk:(c + 1) * tk]


def _in_proj(x2, lw, tabs, B, S):
    T = B * S
    tm, tk = TOKEN_TILE, ATTN_TILE
    s_tiles = S // tm
    nkt = S // tk
    bf = jnp.bfloat16

    def full(a):
        return pl.BlockSpec(a.shape, lambda i, _n=a.ndim: (0,) * _n)

    def rows(width):
        return pl.BlockSpec((tm, width), lambda i: (i, 0))

    def tab_rows(a):
        return pl.BlockSpec((tm, a.shape[1]), lambda i: (i % s_tiles, 0))

    def tab_cols(a):
        return pl.BlockSpec((a.shape[0], tm), lambda i: (0, i % s_tiles))

    def kt_spec(n):
        return pl.BlockSpec((1, n, tm // tk, LANES, tk), lambda i: (i // s_tiles, 0, i % s_tiles, 0, 0))

    weights = [lw['attn_norm'], lw['w_row'], lw['w_t'], lw['q_norm'], lw['kv_norm'],
               lw['w_uq'], lw['w_v'], lw['w_kt']]
    row_tabs = [tabs['cq'], tabs['s1q'], tabs['s2q'], tabs['cc'], tabs['s1c'], tabs['s2c']]
    mla_tabs = [tabs['cm'], tabs['s1m'], tabs['s2m']]
    in_specs = ([rows(D_MODEL)] + [full(w) for w in weights] + [tab_rows(t) for t in row_tabs]
                + [tab_cols(tabs['cos_t']), tab_cols(tabs['sin_t'])]
                + [tab_rows(t) for t in mla_tabs]
                + [tab_cols(tabs['cos_t16']), tab_cols(tabs['sin_t16'])])
    out_shape = [
        jax.ShapeDtypeStruct((T, NSA_HEADS * LANES), bf),
        jax.ShapeDtypeStruct((T, 256), jnp.float32),
        jax.ShapeDtypeStruct((T, NSA_KV_GROUPS * LANES), bf),
        jax.ShapeDtypeStruct((T, NSA_KV_GROUPS * LANES), bf),
        jax.ShapeDtypeStruct((T, 3 * NSA_KV_GROUPS * LANES), jnp.float32),
        jax.ShapeDtypeStruct((T, D_MODEL), bf),
        jax.ShapeDtypeStruct((T, D_MODEL), bf),
        jax.ShapeDtypeStruct((B, NSA_KV_GROUPS, nkt, LANES, tk), bf),
        jax.ShapeDtypeStruct((B, NSA_KV_GROUPS, nkt, LANES, tk), bf),
        jax.ShapeDtypeStruct((T, MLA_HEADS * LANES), bf),
        jax.ShapeDtypeStruct((T, MLA_HEADS * LANES), bf),
        jax.ShapeDtypeStruct((B, MLA_HEADS, nkt, LANES, tk), bf),
    ]
    out_specs = [rows(NSA_HEADS * LANES), rows(256), rows(NSA_KV_GROUPS * LANES), rows(NSA_KV_GROUPS * LANES),
                 rows(3 * NSA_KV_GROUPS * LANES), rows(D_MODEL), rows(D_MODEL),
                 kt_spec(NSA_KV_GROUPS), kt_spec(NSA_KV_GROUPS),
                 rows(MLA_HEADS * LANES), rows(MLA_HEADS * LANES), kt_spec(MLA_HEADS)]
    return pl.pallas_call(
        functools.partial(_in_proj_kernel, tm=tm, tk=tk, s_tiles=s_tiles),
        grid=(T // tm,),
        in_specs=in_specs,
        out_specs=out_specs,
        out_shape=out_shape,
        compiler_params=pltpu.CompilerParams(dimension_semantics=("arbitrary",),
                                             vmem_limit_bytes=VMEM_LIMIT_BYTES),
        name="in_proj",
    )(x2, *weights, *row_tabs, tabs['cos_t'], tabs['sin_t'], *mla_tabs, tabs['cos_t16'], tabs['sin_t16'])


def _compress_kernel(blk_ref, pe_ref, w1_ref, w2_ref, w2t_ref, ot_ref, or_ref):
    tb = (blk_ref[0, 0] + pe_ref[0]).astype(jnp.bfloat16)
    h = _dot(tb, w1_ref[0])
    h = (h * jax.nn.sigmoid(h)).astype(jnp.bfloat16)
    ot_ref[0, 0] = _nt_dot(w2t_ref[0], h).astype(jnp.bfloat16)
    or_ref[0, 0] = _dot(h, w2_ref[0]).astype(jnp.bfloat16)


def _compress(blocks, lw):
    B, _, NC, K = blocks.shape
    bf = jnp.bfloat16
    return pl.pallas_call(
        _compress_kernel,
        grid=(B, 2 * NSA_KV_GROUPS),
        in_specs=[
            pl.BlockSpec((1, 1, NC, K), lambda b, c: (b, c, 0, 0)),
            pl.BlockSpec((1, 1, K), lambda b, c: (c // NSA_KV_GROUPS, 0, 0)),
            pl.BlockSpec((1, K, CMP_HIDDEN), lambda b, c: (c // NSA_KV_GROUPS, 0, 0)),
            pl.BlockSpec((1, CMP_HIDDEN, LANES), lambda b, c: (c // NSA_KV_GROUPS, 0, 0)),
            pl.BlockSpec((1, LANES, CMP_HIDDEN), lambda b, c: (c // NSA_KV_GROUPS, 0, 0)),
        ],
        out_specs=[pl.BlockSpec((1, 1, LANES, NC), lambda b, c: (b, c, 0, 0)),
                   pl.BlockSpec((1, 1, NC, LANES), lambda b, c: (b, c, 0, 0))],
        out_shape=[jax.ShapeDtypeStruct((B, 2 * NSA_KV_GROUPS, LANES, NC), bf),
                   jax.ShapeDtypeStruct((B, 2 * NSA_KV_GROUPS, NC, LANES), bf)],
        compiler_params=pltpu.CompilerParams(dimension_semantics=("arbitrary", "arbitrary"),
                                             vmem_limit_bytes=VMEM_LIMIT_BYTES),
        name="compress",
    )(blocks, lw['cmp_pe'], lw['cmp_w1'], lw['cmp_w2'], lw['cmp_w2t'])


def _cmp_select_kernel(q_ref, kct_ref, vc_ref, gate_ref, ovt_ref, o_ref, selb_ref, score_scr, *, tq, nc, ns):
    qi = pl.program_id(2)
    hd = NSA_HEAD_DIM
    pos = qi * tq + lax.broadcasted_iota(jnp.int32, (tq, nc), 0)
    col = lax.broadcasted_iota(jnp.int32, (tq, nc), 1)
    valid = (col * CMP_STRIDE + (CMP_BLOCK - 1)) <= pos
    lane = lax.broadcasted_iota(jnp.int32, (tq, LANES), 1)
    kct = kct_ref[0, 0]
    vc = vc_ref[0, 0]
    gate = gate_ref[...]

    psum = jnp.zeros((tq, nc), jnp.float32)
    outs = []
    for r in range(NSA_REP):
        s = _dot(q_ref[:, r * LANES:(r + 1) * LANES], kct)
        sf = jnp.where(valid, s, NEG)
        m = jnp.max(sf, axis=1, keepdims=True)
        e = jnp.where(valid, jnp.exp(sf - m), 0.0)
        p = e / jnp.maximum(jnp.sum(e, axis=1, keepdims=True), TINY)
        psum = psum + p
        o = _dot(p.astype(jnp.bfloat16), vc) * gate[:, r:r + 1]
        outs.append(jnp.where(lane < hd, o, 0.0))
    for i in range(NSA_REP // 2):
        pair = outs[2 * i] + pltpu.roll(outs[2 * i + 1], hd, 1)
        o_ref[:, i * LANES:(i + 1) * LANES] = pair.astype(jnp.bfloat16)

    hi = psum.astype(jnp.bfloat16)
    lo = (psum - hi.astype(jnp.float32)).astype(jnp.bfloat16)
    ovt = ovt_ref[...]
    imp = _nt_dot(ovt, hi) + _nt_dot(ovt, lo)
    post = qi * tq + lax.broadcasted_iota(jnp.int32, (ns, tq), 1)
    blk = lax.broadcasted_iota(jnp.int32, (ns, tq), 0)
    cur = post // SLC_BLOCK
    causal = blk <= cur
    forced = (blk == 0) | ((cur - blk >= 0) & (cur - blk < N_LOCAL_FORCED))
    score = jnp.where(causal, imp + jnp.where(forced, FORCE_BONUS, 0.0), NEG)
    score_scr[...] = score

    sub_id = lax.broadcasted_iota(jnp.int32, (SUBLANES, tq), 0)
    counts = []
    for rg in range(ns // SUBLANES):
        sub = score[rg * SUBLANES:(rg + 1) * SUBLANES]
        c = jnp.zeros((SUBLANES, tq), jnp.float32)
        for i in range(ns):
            bi = score_scr[i:i + 1, :]
            if i < rg * SUBLANES:
                ahead = bi >= sub
            elif i >= (rg + 1) * SUBLANES:
                ahead = bi > sub
            else:
                ahead = (bi > sub) | ((bi == sub) & (sub_id > (i - rg * SUBLANES)))
            c = c + jnp.where(ahead, 1.0, 0.0)
        counts.append(c)
    cnt = jnp.concatenate(counts, axis=0)
    sel = (cnt < float(min(SLC_TOPK, ns))) & causal
    bias_t = jnp.where(sel, 0.0, NEG)
    parts = [jnp.zeros((hd, tq), jnp.float32), bias_t]
    if LANES - hd - ns > 0:
        parts.append(jnp.zeros((LANES - hd - ns, tq), jnp.float32))
    selb_ref[...] = jnp.concatenate(parts, axis=0).T.astype(jnp.bfloat16)


def _cmp_select(qn, kct, vc, ng, ovt, B, S):
    T = B * S
    tq = ATTN_TILE
    nq = S // tq
    nc = kct.shape[-1]
    ns = S // SLC_BLOCK
    G = NSA_KV_GROUPS
    bf = jnp.bfloat16
    return pl.pallas_call(
        functools.partial(_cmp_select_kernel, tq=tq, nc=nc, ns=ns),
        grid=(B, G, nq),
        in_specs=[
            pl.BlockSpec((tq, NSA_REP * LANES), lambda b, g, i: (b * nq + i, g)),
            pl.BlockSpec((1, 1, LANES, nc), lambda b, g, i: (b, g, 0, 0)),
            pl.BlockSpec((1, 1, nc, LANES), lambda b, g, i: (b, G + g, 0, 0)),
            pl.BlockSpec((tq, LANES), lambda b, g, i: (b * nq + i, g)),
            pl.BlockSpec((ns, nc), lambda b, g, i: (0, 0)),
        ],
        out_specs=[pl.BlockSpec((tq, NSA_REP * NSA_HEAD_DIM), lambda b, g, i: (b * nq + i, g)),
                   pl.BlockSpec((tq, LANES), lambda b, g, i: (b * nq + i, g))],
        out_shape=[jax.ShapeDtypeStruct((T, NSA_HEADS * NSA_HEAD_DIM), bf),
                   jax.ShapeDtypeStruct((T, G * LANES), bf)],
        scratch_shapes=[pltpu.VMEM((ns, tq), jnp.float32)],
        compiler_params=pltpu.CompilerParams(dimension_semantics=("arbitrary",) * 3,
                                             vmem_limit_bytes=VMEM_LIMIT_BYTES),
        name="cmp_select",
    )(qn, kct, vc, ng, ovt)


def _flash_kernel(*refs, hpk, kv_per_step, tq, tk, window, has_sel, has_gate):
    refs = list(refs)
    q_ref = refs.pop(0)
    selb_ref = refs.pop(0) if has_sel else None
    kt_ref = refs.pop(0)
    v_ref = refs.pop(0)
    gate_ref = refs.pop(0) if has_gate else None
    o_ref, q_scr, m_scr, acc_scr = refs
    nh = hpk * kv_per_step
    hd = NSA_HEAD_DIM
    qi = pl.program_id(2)

    for h in range(nh):
        q = q_ref[:, h * LANES:(h + 1) * LANES]
        if has_sel:
            q = q + selb_ref[...]
        q_scr[h] = q
    m_scr[...] = jnp.full(m_scr.shape, NEG, jnp.float32)
    acc_scr[...] = jnp.zeros(acc_scr.shape, jnp.float32)

    def step(kt, masked):
        if masked:
            qpos = qi * tq + lax.broadcasted_iota(jnp.int32, (tq, tk), 0)
            kpos = kt * tk + lax.broadcasted_iota(jnp.int32, (tq, tk), 1)
            mask = kpos <= qpos
            if window is not None:
                mask = mask & ((qpos - kpos) < window)
        for p in range(kv_per_step):
            ktile = kt_ref[0, p, kt]
            vtile = v_ref[pl.ds(pl.multiple_of(kt * tk, tk), tk), p * LANES:(p + 1) * LANES]
            for r in range(hpk):
                h = p * hpk + r
                s = _dot(q_scr[h], ktile)
                if masked:
                    s = jnp.where(mask, s, NEG)
                m_prev = m_scr[h]
                m_new = jnp.maximum(m_prev, jnp.max(s, axis=1, keepdims=True))
                alpha = jnp.exp(m_prev - m_new)
                e = jnp.exp(s - jnp.tile(m_new, (1, tk // LANES)))
                if masked:
                    e = jnp.where(mask, e, 0.0)
                pv = _dot(e.astype(jnp.bfloat16), vtile)
                acc_scr[h] = alpha * acc_scr[h] + pv
                m_scr[h] = m_new

    if window is None:
        lo = 0
        def body(kt, carry):
            step(kt, False)
            return carry
    else:
        lo = jnp.maximum(qi - window // tk, 0)
        def body(kt, carry):
            step(kt, True)
            return carry
    lax.fori_loop(lo, qi, body, 0)
    step(qi, True)

    lane = lax.broadcasted_iota(jnp.int32, (tq, LANES), 1)
    outs = []
    for h in range(nh):
        acc = acc_scr[h]
        o = acc / jnp.maximum(acc[:, hd:hd + 1], TINY)
        if has_gate:
            o = o * gate_ref[:, h:h + 1]
        outs.append(jnp.where(lane < hd, o, 0.0))
    for i in range(nh // 2):
        pair = outs[2 * i] + pltpu.roll(outs[2 * i + 1], hd, 1)
        o_ref[:, i * LANES:(i + 1) * LANES] = pair.astype(jnp.bfloat16)


def _flash(q, kt, v, B, S, *, hpk, kv_per_step, window=None, selb=None, gate=None, gate_col=0, name):
    T = B * S
    tq = tk = ATTN_TILE
    nq = S // tq
    nkv = kt.shape[1]
    nh = hpk * kv_per_step
    steps = nkv // kv_per_step
    hd = NSA_HEAD_DIM
    in_specs = [pl.BlockSpec((tq, nh * LANES), lambda b, j, i: (b * nq + i, j))]
    args = [q]
    if selb is not None:
        in_specs.append(pl.BlockSpec((tq, LANES), lambda b, j, i: (b * nq + i, j)))
        args.append(selb)
    in_specs.append(pl.BlockSpec((1, kv_per_step, S // tk, LANES, tk), lambda b, j, i: (b, j, 0, 0, 0)))
    in_specs.append(pl.BlockSpec((S, kv_per_step * LANES), lambda b, j, i: (b, j)))
    args += [kt, v]
    if gate is not None:
        in_specs.append(pl.BlockSpec((tq, LANES), lambda b, j, i: (b * nq + i, gate_col + j)))
        args.append(gate)
    return pl.pallas_call(
        functools.partial(_flash_kernel, hpk=hpk, kv_per_step=kv_per_step, tq=tq, tk=tk, window=window,
                          has_sel=selb is not None, has_gate=gate is not None),
        grid=(B, steps, nq),
        in_specs=in_specs,
        out_specs=pl.BlockSpec((tq, nh * hd), lambda b, j, i: (b * nq + i, j)),
        out_shape=jax.ShapeDtypeStruct((T, nkv * hpk * hd), jnp.bfloat16),
        scratch_shapes=[pltpu.VMEM((nh, tq, LANES), jnp.bfloat16),
                        pltpu.VMEM((nh, tq, LANES), jnp.float32),
                        pltpu.VMEM((nh, tq, LANES), jnp.float32)],
        compiler_params=pltpu.CompilerParams(dimension_semantics=("arbitrary",) * 3,
                                             vmem_limit_bytes=VMEM_LIMIT_BYTES),
        name=name,
    )(*args)


def _merge_kernel(oc_ref, os_ref, ow_ref, ob_ref, ga_ref, gb_ref, x_ref, wa_ref, wb_ref, wo_ref, out_ref):
    f32 = jnp.float32
    o_a = (oc_ref[...].astype(f32) + os_ref[...].astype(f32) + ow_ref[...].astype(f32)).astype(jnp.bfloat16)
    merged = (ga_ref[...].astype(f32) * _dot(o_a, wa_ref[...])
              + gb_ref[...].astype(f32) * _dot(ob_ref[...], wb_ref[...]))
    out_ref[...] = x_ref[...] + _dot(merged.astype(jnp.bfloat16), wo_ref[...])


def _merge(o_cmp, o_slc, o_win, o_b, ga, gb, x2, lw):
    T = x2.shape[0]
    tm = TOKEN_TILE

    def rows(width):
        return pl.BlockSpec((tm, width), lambda i: (i, 0))

    def full(a):
        return pl.BlockSpec(a.shape, lambda i: (0, 0))

    w = NSA_HEADS * NSA_HEAD_DIM
    return pl.pallas_call(
        _merge_kernel,
        grid=(T // tm,),
        in_specs=[rows(w), rows(w), rows(w), rows(MLA_HEADS * MLA_V), rows(D_MODEL), rows(D_MODEL),
                  rows(D_MODEL), full(lw['nsa_w_o']), full(lw['mla_w_o']), full(lw['w_out'])],
        out_specs=rows(D_MODEL),
        out_shape=jax.ShapeDtypeStruct((T, D_MODEL), jnp.float32),
        compiler_params=pltpu.CompilerParams(dimension_semantics=("arbitrary",),
                                             vmem_limit_bytes=VMEM_LIMIT_BYTES),
        name="merge",
    )(o_cmp, o_slc, o_win, o_b, ga, gb, x2, lw['nsa_w_o'], lw['mla_w_o'], lw['w_out'])


def _mlp_kernel(x_ref, g_ref, wu_ref, wd_ref, gf_ref, out_ref, *, ff_chunk, final):
    x = x_ref[...]
    hn = _rms(x, g_ref[...]).astype(jnp.bfloat16)
    acc = jnp.zeros(x.shape, jnp.float32)
    for c in range(D_FF // ff_chunk):
        h = jnp.maximum(_dot(hn, wu_ref[:, c * ff_chunk:(c + 1) * ff_chunk]), 0.0)
        acc = acc + _dot((h * h).astype(jnp.bfloat16), wd_ref[c * ff_chunk:(c + 1) * ff_chunk, :])
    y = x + acc
    if final:
        y = _rms(y, gf_ref[...])
    out_ref[...] = y


def _mlp(x2, lw, final_norm, final):
    T = x2.shape[0]
    tm = TOKEN_TILE

    def full(a):
        return pl.BlockSpec(a.shape, lambda i: (0, 0))

    return pl.pallas_call(
        functools.partial(_mlp_kernel, ff_chunk=1024, final=final),
        grid=(T // tm,),
        in_specs=[pl.BlockSpec((tm, D_MODEL), lambda i: (i, 0)), full(lw['mlp_norm']),
                  full(lw['w_up']), full(lw['w_down']), full(final_norm)],
        out_specs=pl.BlockSpec((tm, D_MODEL), lambda i: (i, 0)),
        out_shape=jax.ShapeDtypeStruct((T, D_MODEL), jnp.float32),
        compiler_params=pltpu.CompilerParams(dimension_semantics=("arbitrary",),
                                             vmem_limit_bytes=VMEM_LIMIT_BYTES),
        name="mlp",
    )(x2, lw['mlp_norm'], lw['w_up'], lw['w_down'], final_norm)


def _pad_cols(w, width):
    return jnp.pad(w, ((0, 0), (0, width - w.shape[1])))


def _prep_layer(p, l):
    bf = jnp.bfloat16
    hd = NSA_HEAD_DIM
    G = NSA_KV_GROUPS
    w = p['w_in'][l]
    cols = []
    for h in range(NSA_HEADS):
        cols.append(_pad_cols(w[:, _OFF_Q + h * hd:_OFF_Q + (h + 1) * hd], LANES))
    cols.append(w[:, _OFF_KVC:_OFF_KVS])
    for off in (_OFF_KVS, _OFF_KVW):
        for g in range(G):
            cols.append(_pad_cols(w[:, off + (G + g) * hd:off + (G + g + 1) * hd], LANES))
    for br in range(3):
        for g in range(G):
            c0 = _OFF_NG + br * NSA_HEADS + g * NSA_REP
            cols.append(_pad_cols(w[:, c0:c0 + NSA_REP], LANES))
    cols.append(w[:, _OFF_CQ:_OFF_CKV])
    cols.append(w[:, _OFF_CKV:_OFF_KR])
    cols.append(w[:, _OFF_GA:_OFF_GB])
    cols.append(w[:, _OFF_GB:_OFF_GB + D_MODEL])
    w_row = jnp.concatenate(cols, axis=1).astype(bf)
    t_cols = [w[:, off + g * hd:off + (g + 1) * hd] for off in (_OFF_KVS, _OFF_KVW) for g in range(G)]
    t_cols.append(w[:, _OFF_KR:_OFF_GA])
    w_t = jnp.concatenate(t_cols, axis=1).T.astype(bf)

    dq = MLA_NOPE + MLA_ROPE
    wuq = p['mla_w_uq'][l]
    w_uq = jnp.concatenate([_pad_cols(wuq[:, h * dq:(h + 1) * dq], LANES) for h in range(MLA_HEADS)], axis=1)
    wukv = p['mla_w_ukv'][l]
    dkv = MLA_NOPE + MLA_V
    w_v = jnp.concatenate([_pad_cols(wukv[:, h * dkv + MLA_NOPE:(h + 1) * dkv], LANES)
                           for h in range(MLA_HEADS)], axis=1)
    w_kt = jnp.concatenate([wukv[:, h * dkv:h * dkv + MLA_NOPE] for h in range(MLA_HEADS)], axis=1).T

    cmp_w2 = p['cmp_w2'][l]
    cmp_w2p = jnp.pad(cmp_w2, ((0, 0), (0, 0), (0, LANES - hd)))
    return {
        'attn_norm': p['attn_norm'][l][None, :],
        'w_row': w_row, 'w_t': w_t,
        'q_norm': p['mla_q_norm'][l][None, :], 'kv_norm': p['mla_kv_norm'][l][None, :],
        'w_uq': w_uq.astype(bf), 'w_v': w_v.astype(bf), 'w_kt': w_kt.astype(bf),
        'cmp_pe': p['cmp_pe'][l].reshape(2, 1, CMP_BLOCK * hd),
        'cmp_w1': p['cmp_w1'][l].astype(bf),
        'cmp_w2': cmp_w2p.astype(bf),
        'cmp_w2t': jnp.swapaxes(cmp_w2p, 1, 2).astype(bf),
        'nsa_w_o': p['nsa_w_o'][l].astype(bf), 'mla_w_o': p['mla_w_o'][l].astype(bf),
        'w_out': p['w_out'][l].astype(bf),
        'mlp_norm': p['mlp_norm'][l][None, :],
        'w_up': p['w_up'][l].astype(bf), 'w_down': p['w_down'][l].astype(bf),
    }


def _tables(S):
    pos = jnp.arange(S, dtype=jnp.int32).astype(jnp.float32)
    hd = NSA_HEAD_DIM

    def angles(d):
        inv = ROPE_THETA ** (-jnp.arange(0, d, 2, dtype=jnp.float32) / d)
        ang = pos[:, None] * inv[None, :]
        return jnp.cos(ang), jnp.sin(ang)

    cos, sin = angles(hd)
    z = jnp.zeros_like(cos)
    scale = hd ** -0.5
    pad = jnp.zeros((S, LANES - hd), jnp.float32)
    cq = jnp.concatenate([cos, cos, pad], axis=1) * scale
    s1q = jnp.concatenate([z, sin, pad], axis=1) * scale
    s2q = jnp.concatenate([-sin, z, pad], axis=1) * scale
    cc = jnp.concatenate([cos, cos, cos, cos], axis=1)
    s1c = jnp.concatenate([z, sin, z, sin], axis=1)
    s2c = jnp.concatenate([-sin, z, -sin, z], axis=1)

    cos16, sin16 = angles(MLA_ROPE)
    z16 = jnp.zeros_like(cos16)
    sm = (MLA_NOPE + MLA_ROPE) ** -0.5
    one = jnp.ones((S, MLA_NOPE), jnp.float32)
    z64 = jnp.zeros((S, MLA_NOPE), jnp.float32)
    pad32 = jnp.zeros((S, LANES - MLA_NOPE - MLA_ROPE), jnp.float32)
    cm = jnp.concatenate([one, cos16, cos16, pad32], axis=1) * sm
    s1m = jnp.concatenate([z64, z16, sin16, pad32], axis=1) * sm
    s2m = jnp.concatenate([z64, -sin16, z16, pad32], axis=1) * sm
    return {'cq': cq, 's1q': s1q, 's2q': s2q, 'cc': cc, 's1c': s1c, 's2c': s2c,
            'cos_t': cos.T, 'sin_t': sin.T, 'cm': cm, 's1m': s1m, 's2m': s2m,
            'cos_t16': cos16.T, 'sin_t16': sin16.T}


def _overlap_t(S):
    nc = S // CMP_STRIDE
    n_cmp = (S - CMP_BLOCK) // CMP_STRIDE + 1
    ns = S // SLC_BLOCK
    start = np.arange(nc) * CMP_STRIDE
    sel = np.arange(ns) * SLC_BLOCK
    lo = np.maximum(start[None, :], sel[:, None])
    hi = np.minimum(start[None, :] + CMP_BLOCK, sel[:, None] + SLC_BLOCK)
    ov = np.clip(hi - lo, 0, None) / CMP_BLOCK
    ov[:, n_cmp:] = 0.0
    return jnp.asarray(ov, jnp.bfloat16)


def _cmp_blocks(cmp_in, B, S):
    hd = NSA_HEAD_DIM
    nc = S // CMP_STRIDE
    r = cmp_in.reshape(B, nc, CMP_STRIDE, 2 * NSA_KV_GROUPS, hd)
    r = r.transpose(0, 3, 1, 2, 4).reshape(B, 2 * NSA_KV_GROUPS, nc, CMP_STRIDE * hd)
    nxt = jnp.concatenate([r[:, :, 1:], jnp.zeros_like(r[:, :, :1])], axis=2)
    return jnp.concatenate([r, nxt], axis=-1)


def _forward(x, params, depth):
    B, S, D = x.shape
    T = B * S
    tabs = _tables(S)
    ovt = _overlap_t(S)
    x2 = x.reshape(T, D)
    final_norm = params['final_norm'][None, :]
    G = NSA_KV_GROUPS
    for l in range(depth):
        lw = _prep_layer(params, l)
        (qn, cmp_in, vs, vw, ng, ga, gb, kts, ktw, qm, vm, ktm) = _in_proj(x2, lw, tabs, B, S)
        kct, vc = _compress(_cmp_blocks(cmp_in, B, S), lw)
        o_cmp, selb = _cmp_select(qn, kct, vc, ng, ovt, B, S)
        o_slc = _flash(qn, kts, vs, B, S, hpk=NSA_REP, kv_per_step=1, selb=selb,
                       gate=ng, gate_col=G, name="flash_slc")
        o_win = _flash(qn, ktw, vw, B, S, hpk=NSA_REP, kv_per_step=1, window=WINDOW,
                       gate=ng, gate_col=2 * G, name="flash_win")
        o_b = _flash(qm, ktm, vm, B, S, hpk=1, kv_per_step=2, name="flash_mla")
        x2 = _merge(o_cmp, o_slc, o_win, o_b, ga, gb, x2, lw)
        x2 = _mlp(x2, lw, final_norm, final=(l == depth - 1))
    return x2.reshape(B, S, D)


def kernel(x, attn_norm, w_in, cmp_pe, cmp_w1, cmp_w2, nsa_w_o, mla_q_norm, mla_kv_norm, mla_w_uq,
           mla_w_ukv, mla_w_o, w_out, mlp_norm, w_up, w_down, final_norm):
    params = dict(attn_norm=attn_norm, w_in=w_in, cmp_pe=cmp_pe, cmp_w1=cmp_w1, cmp_w2=cmp_w2,
                  nsa_w_o=nsa_w_o, mla_q_norm=mla_q_norm, mla_kv_norm=mla_kv_norm, mla_w_uq=mla_w_uq,
                  mla_w_ukv=mla_w_ukv, mla_w_o=mla_w_o, w_out=w_out, mlp_norm=mlp_norm, w_up=w_up,
                  w_down=w_down, final_norm=final_norm)
    return _forward(x, params, w_in.shape[0])
```

```python
import functools

import numpy as np
import jax
import jax.numpy as jnp
from jax import lax
from jax.experimental import pallas as pl
from jax.experimental.pallas import tpu as pltpu

EPS = 1e-6
ROPE_THETA = 10000.0
NEG = -1e30
TINY = 1e-30

D_MODEL = 1024
NSA_HEADS = 8
NSA_KV_GROUPS = 2
NSA_REP = NSA_HEADS // NSA_KV_GROUPS
NSA_HEAD_DIM = 64
CMP_BLOCK = 32
CMP_STRIDE = 16
CMP_HIDDEN = 4 * NSA_HEAD_DIM
SLC_BLOCK = 64
SLC_TOPK = 16
N_LOCAL_FORCED = 2
FORCE_BONUS = 1e4
WINDOW = 512

MLA_HEADS = 8
MLA_Q_LORA = 256
MLA_KV_LORA = 128
MLA_NOPE = 64
MLA_ROPE = 32
MLA_V = 64
D_FF = 4 * D_MODEL

LANES = 128
SUBLANES = 8
VMEM_LIMIT_BYTES = 56 * 1024 * 1024

TOKEN_TILE = 512
FLASH_TILE = 512
CMP_TILE = 256

_OFF_Q = 0
_OFF_KVC = _OFF_Q + NSA_HEADS * NSA_HEAD_DIM
_OFF_KVS = _OFF_KVC + 2 * NSA_KV_GROUPS * NSA_HEAD_DIM
_OFF_KVW = _OFF_KVS + 2 * NSA_KV_GROUPS * NSA_HEAD_DIM
_OFF_NG = _OFF_KVW + 2 * NSA_KV_GROUPS * NSA_HEAD_DIM
_OFF_CQ = _OFF_NG + 3 * NSA_HEADS
_OFF_CKV = _OFF_CQ + MLA_Q_LORA
_OFF_KR = _OFF_CKV + MLA_KV_LORA
_OFF_GA = _OFF_KR + MLA_ROPE
_OFF_GB = _OFF_GA + D_MODEL

_R_Q = 0
_R_CMP = _R_Q + NSA_HEADS * LANES
_R_VS = _R_CMP + 256
_R_VW = _R_VS + NSA_KV_GROUPS * LANES
_R_NG = _R_VW + NSA_KV_GROUPS * LANES
_R_CQ = _R_NG + 3 * NSA_KV_GROUPS * LANES
_R_CKV = _R_CQ + MLA_Q_LORA
_R_GA = _R_CKV + MLA_KV_LORA
_R_GB = _R_GA + D_MODEL
_R_END = _R_GB + D_MODEL


def _nt_dot(a, b):
    return lax.dot_general(a, b, (((1,), (1,)), ((), ())), preferred_element_type=jnp.float32)


def _dot(a, b):
    return jnp.dot(a, b, preferred_element_type=jnp.float32)


def _rms(xf, g):
    return xf * lax.rsqrt(jnp.mean(xf * xf, axis=-1, keepdims=True) + EPS) * g


def _rope_rows(y, c, s1, s2, sh):
    return y * c + pltpu.roll(y, sh, 1) * s1 + pltpu.roll(y, LANES - sh, 1) * s2


def _rope_cols(y, cos_t, sin_t):
    h = y.shape[0] // 2
    a, b = y[:h], y[h:]
    return jnp.concatenate([a * cos_t - b * sin_t, a * sin_t + b * cos_t], axis=0)


def _in_proj_kernel(x_ref, g_ref, w_ref, wt_ref, qng_ref, kvng_ref, wuq_ref, wv_ref, wkt_ref,
                    cq_ref, s1q_ref, s2q_ref, cc_ref, s1c_ref, s2c_ref, cost_ref, sint_ref,
                    cm_ref, s1m_ref, s2m_ref, cost16_ref, sint16_ref,
                    qn_ref, cmp_ref, vs_ref, vw_ref, ng_ref, ga_ref, gb_ref, kts_ref, ktw_ref,
                    qm_ref, vm_ref, ktm_ref, *, tm, tk, s_tiles):
    si = pl.program_id(0) % s_tiles
    hd = NSA_HEAD_DIM
    xn = _rms(x_ref[...], g_ref[...]).astype(jnp.bfloat16)

    lane = lax.broadcasted_iota(jnp.int32, (tm, LANES), 1)
    ones_col = jnp.where(lane == hd, 1.0, 0.0)

    y = _dot(xn, w_ref[:, _R_Q:_R_CMP])
    cq, s1q, s2q = cq_ref[...], s1q_ref[...], s2q_ref[...]
    for h in range(NSA_HEADS):
        blk = y[:, h * LANES:(h + 1) * LANES]
        qn_ref[:, h * LANES:(h + 1) * LANES] = _rope_rows(blk, cq, s1q, s2q, hd // 2).astype(jnp.bfloat16)

    y = _dot(xn, w_ref[:, _R_CMP:_R_VS])
    cmp_ref[:, :LANES] = _rope_rows(y[:, :LANES], cc_ref[...], s1c_ref[...], s2c_ref[...], hd // 2)
    cmp_ref[:, LANES:] = y[:, LANES:]

    y = _dot(xn, w_ref[:, _R_VS:_R_VW])
    for g in range(NSA_KV_GROUPS):
        vs_ref[:, g * LANES:(g + 1) * LANES] = (y[:, g * LANES:(g + 1) * LANES] + ones_col).astype(jnp.bfloat16)
    y = _dot(xn, w_ref[:, _R_VW:_R_NG])
    for g in range(NSA_KV_GROUPS):
        vw_ref[:, g * LANES:(g + 1) * LANES] = (y[:, g * LANES:(g + 1) * LANES] + ones_col).astype(jnp.bfloat16)

    ng_ref[...] = jax.nn.sigmoid(_dot(xn, w_ref[:, _R_NG:_R_CQ]))
    ga_ref[...] = jax.nn.sigmoid(_dot(xn, w_ref[:, _R_GA:_R_GB])).astype(jnp.bfloat16)
    gb_ref[...] = jax.nn.sigmoid(_dot(xn, w_ref[:, _R_GB:_R_END])).astype(jnp.bfloat16)

    cqn = _rms(_dot(xn, w_ref[:, _R_CQ:_R_CKV]), qng_ref[...]).astype(jnp.bfloat16)
    y = _dot(cqn, wuq_ref[...])
    cm, s1m, s2m = cm_ref[...], s1m_ref[...], s2m_ref[...]
    for h in range(MLA_HEADS):
        blk = y[:, h * LANES:(h + 1) * LANES]
        qm_ref[:, h * LANES:(h + 1) * LANES] = _rope_rows(blk, cm, s1m, s2m, MLA_ROPE // 2).astype(jnp.bfloat16)

    ckvn = _rms(_dot(xn, w_ref[:, _R_CKV:_R_GA]), kvng_ref[...]).astype(jnp.bfloat16)
    y = _dot(ckvn, wv_ref[...])
    for h in range(MLA_HEADS):
        vm_ref[:, h * LANES:(h + 1) * LANES] = (y[:, h * LANES:(h + 1) * LANES] + ones_col).astype(jnp.bfloat16)
    knt = _nt_dot(wkt_ref[...], ckvn)

    yt = _nt_dot(wt_ref[...], xn)
    cos_t, sin_t = cost_ref[...], sint_ref[...]
    kpos = si * tm + lax.broadcasted_iota(jnp.int32, (SLC_BLOCK, tm), 1)
    blk_id = lax.broadcasted_iota(jnp.int32, (SLC_BLOCK, tm), 0)
    sel_rows = jnp.where((kpos // SLC_BLOCK) == blk_id, 1.0, 0.0)
    zero_rows = jnp.zeros((LANES - hd, tm), jnp.float32)
    for g in range(NSA_KV_GROUPS):
        ks = _rope_cols(yt[g * hd:(g + 1) * hd], cos_t, sin_t)
        kw = _rope_cols(yt[(NSA_KV_GROUPS + g) * hd:(NSA_KV_GROUPS + g + 1) * hd], cos_t, sin_t)
        ks = jnp.concatenate([ks, sel_rows], axis=0).astype(jnp.bfloat16)
        kw = jnp.concatenate([kw, zero_rows], axis=0).astype(jnp.bfloat16)
        for c in range(tm // tk):
            kts_ref[0, g, c] = ks[:, c * tk:(c + 1) * tk]
            ktw_ref[0, g, c] = kw[:, c * tk:(c + 1) * tk]
    kpe = _rope_cols(yt[2 * NSA_KV_GROUPS * hd:], cost16_ref[...], sint16_ref[...])
    pad_rows = jnp.zeros((LANES - MLA_NOPE - MLA_ROPE, tm), jnp.float32)
    for h in range(MLA_HEADS):
        km = jnp.concatenate([knt[h * MLA_NOPE:(h + 1) * MLA_NOPE], kpe, pad_rows], axis=0).astype(jnp.bfloat16)
        for c in range(tm // tk):
            ktm_ref[0, h, c] = km[:, c * tk:(c + 1) * tk]


def _in_proj(x2, lw, tabs, B, S):
    T = B * S
    tm, tk = TOKEN_TILE, FLASH_TILE
    s_tiles = S // tm
    nkt = S // tk
    bf = jnp.bfloat16

    def full(a):
        return pl.BlockSpec(a.shape, lambda i, _n=a.ndim: (0,) * _n)

    def rows(width):
        return pl.BlockSpec((tm, width), lambda i: (i, 0))

    def tab_rows(a):
        return pl.BlockSpec((tm, a.shape[1]), lambda i: (i % s_tiles, 0))

    def tab_cols(a):
        return pl.BlockSpec((a.shape[0], tm), lambda i: (0, i % s_tiles))

    def kt_spec(n):
        return pl.BlockSpec((1, n, tm // tk, LANES, tk), lambda i: (i // s_tiles, 0, i % s_tiles, 0, 0))

    weights = [lw['attn_norm'], lw['w_row'], lw['w_t'], lw['q_norm'], lw['kv_norm'],
               lw['w_uq'], lw['w_v'], lw['w_kt']]
    row_tabs = [tabs['cq'], tabs['s1q'], tabs['s2q'], tabs['cc'], tabs['s1c'], tabs['s2c']]
    mla_tabs = [tabs['cm'], tabs['s1m'], tabs['s2m']]
    in_specs = ([rows(D_MODEL)] + [full(w) for w in weights] + [tab_rows(t) for t in row_tabs]
                + [tab_cols(tabs['cos_t']), tab_cols(tabs['sin_t'])]
                + [tab_rows(t) for t in mla_tabs]
                + [tab_cols(tabs['cos_t16']), tab_cols(tabs['sin_t16'])])
    out_shape = [
        jax.ShapeDtypeStruct((T, NSA_HEADS * LANES), bf),
        jax.ShapeDtypeStruct((T, 256), jnp.float32),
        jax.ShapeDtypeStruct((T, NSA_KV_GROUPS * LANES), bf),
        jax.ShapeDtypeStruct((T, NSA_KV_GROUPS * LANES), bf),
        jax.ShapeDtypeStruct((T, 3 * NSA_KV_GROUPS * LANES), jnp.float32),
        jax.ShapeDtypeStruct((T, D_MODEL), bf),
        jax.ShapeDtypeStruct((T, D_MODEL), bf),
        jax.ShapeDtypeStruct((B, NSA_KV_GROUPS, nkt, LANES, tk), bf),
        jax.ShapeDtypeStruct((B, NSA_KV_GROUPS, nkt, LANES, tk), bf),
        jax.ShapeDtypeStruct((T, MLA_HEADS * LANES), bf),
        jax.ShapeDtypeStruct((T, MLA_HEADS * LANES), bf),
        jax.ShapeDtypeStruct((B, MLA_HEADS, nkt, LANES, tk), bf),
    ]
    out_specs = [rows(NSA_HEADS * LANES), rows(256), rows(NSA_KV_GROUPS * LANES), rows(NSA_KV_GROUPS * LANES),
                 rows(3 * NSA_KV_GROUPS * LANES), rows(D_MODEL), rows(D_MODEL),
                 kt_spec(NSA_KV_GROUPS), kt_spec(NSA_KV_GROUPS),
                 rows(MLA_HEADS * LANES), rows(MLA_HEADS * LANES), kt_spec(MLA_HEADS)]
    return pl.pallas_call(
        functools.partial(_in_proj_kernel, tm=tm, tk=tk, s_tiles=s_tiles),
        grid=(T // tm,),
        in_specs=in_specs,
        out_specs=out_specs,
        out_shape=out_shape,
        compiler_params=pltpu.CompilerParams(dimension_semantics=("arbitrary",),
                                             vmem_limit_bytes=VMEM_LIMIT_BYTES),
        name="in_proj",
    )(x2, *weights, *row_tabs, tabs['cos_t'], tabs['sin_t'], *mla_tabs, tabs['cos_t16'], tabs['sin_t16'])


def _compress_kernel(blk_ref, pe_ref, w1_ref, w2_ref, w2t_ref, ot_ref, or_ref):
    tb = (blk_ref[0, 0] + pe_ref[0]).astype(jnp.bfloat16)
    h = _dot(tb, w1_ref[0])
    h = (h * jax.nn.sigmoid(h)).astype(jnp.bfloat16)
    ot_ref[0, 0] = _nt_dot(w2t_ref[0], h).astype(jnp.bfloat16)
    or_ref[0, 0] = _dot(h, w2_ref[0]).astype(jnp.bfloat16)


def _compress(blocks, lw):
    B, _, NC, K = blocks.shape
    bf = jnp.bfloat16
    return pl.pallas_call(
        _compress_kernel,
        grid=(B, 2 * NSA_KV_GROUPS),
        in_specs=[
            pl.BlockSpec((1, 1, NC, K), lambda b, c: (b, c, 0, 0)),
            pl.BlockSpec((1, 1, K), lambda b, c: (c // NSA_KV_GROUPS, 0, 0)),
            pl.BlockSpec((1, K, CMP_HIDDEN), lambda b, c: (c // NSA_KV_GROUPS, 0, 0)),
            pl.BlockSpec((1, CMP_HIDDEN, LANES), lambda b, c: (c // NSA_KV_GROUPS, 0, 0)),
            pl.BlockSpec((1, LANES, CMP_HIDDEN), lambda b, c: (c // NSA_KV_GROUPS, 0, 0)),
        ],
        out_specs=[pl.BlockSpec((1, 1, LANES, NC), lambda b, c: (b, c, 0, 0)),
                   pl.BlockSpec((1, 1, NC, LANES), lambda b, c: (b, c, 0, 0))],
        out_shape=[jax.ShapeDtypeStruct((B, 2 * NSA_KV_GROUPS, LANES, NC), bf),
                   jax.ShapeDtypeStruct((B, 2 * NSA_KV_GROUPS, NC, LANES), bf)],
        compiler_params=pltpu.CompilerParams(dimension_semantics=("arbitrary", "arbitrary"),
                                             vmem_limit_bytes=VMEM_LIMIT_BYTES),
        name="compress",
    )(blocks, lw['cmp_pe'], lw['cmp_w1'], lw['cmp_w2'], lw['cmp_w2t'])


def _cmp_select_kernel(q_ref, kct_ref, vc_ref, gate_ref, ovt_ref, o_ref, selb_ref, score_scr, *, tq, nc, ns):
    qi = pl.program_id(2)
    hd = NSA_HEAD_DIM
    pos = qi * tq + lax.broadcasted_iota(jnp.int32, (tq, nc), 0)
    col = lax.broadcasted_iota(jnp.int32, (tq, nc), 1)
    valid = (col * CMP_STRIDE + (CMP_BLOCK - 1)) <= pos
    lane = lax.broadcasted_iota(jnp.int32, (tq, LANES), 1)
    kct = kct_ref[0, 0]
    vc = vc_ref[0, 0]
    gate = gate_ref[...]

    psum = jnp.zeros((tq, nc), jnp.float32)
    outs = []
    for r in range(NSA_REP):
        s = _dot(q_ref[:, r * LANES:(r + 1) * LANES], kct)
        sf = jnp.where(valid, s, NEG)
        m = jnp.max(sf, axis=1, keepdims=True)
        e = jnp.where(valid, jnp.exp(sf - m), 0.0)
        p = e / jnp.maximum(jnp.sum(e, axis=1, keepdims=True), TINY)
        psum = psum + p
        o = _dot(p.astype(jnp.bfloat16), vc) * gate[:, r:r + 1]
        outs.append(jnp.where(lane < hd, o, 0.0))
    for i in range(NSA_REP // 2):
        pair = outs[2 * i] + pltpu.roll(outs[2 * i + 1], hd, 1)
        o_ref[:, i * LANES:(i + 1) * LANES] = pair.astype(jnp.bfloat16)

    hi = psum.astype(jnp.bfloat16)
    lo = (psum - hi.astype(jnp.float32)).astype(jnp.bfloat16)
    ovt = ovt_ref[...]
    imp = _nt_dot(ovt, hi) + _nt_dot(ovt, lo)
    post = qi * tq + lax.broadcasted_iota(jnp.int32, (ns, tq), 1)
    blk = lax.broadcasted_iota(jnp.int32, (ns, tq), 0)
    cur = post // SLC_BLOCK
    causal = blk <= cur
    forced = (blk == 0) | ((cur - blk >= 0) & (cur - blk < N_LOCAL_FORCED))
    score = jnp.where(causal, imp + jnp.where(forced, FORCE_BONUS, 0.0), NEG)
    score_scr[...] = score

    sub_id = lax.broadcasted_iota(jnp.int32, (SUBLANES, tq), 0)
    counts = []
    for rg in range(ns // SUBLANES):
        sub = score[rg * SUBLANES:(rg + 1) * SUBLANES]
        c = jnp.zeros((SUBLANES, tq), jnp.float32)
        for i in range(ns):
            bi = score_scr[i:i + 1, :]
            if i < rg * SUBLANES:
                ahead = bi >= sub
            elif i >= (rg + 1) * SUBLANES:
                ahead = bi > sub
            else:
                ahead = (bi > sub) | ((bi == sub) & (sub_id > (i - rg * SUBLANES)))
            c = c + jnp.where(ahead, 1.0, 0.0)
        counts.append(c)
    cnt = jnp.concatenate(counts, axis=0)
    sel = (cnt < float(min(SLC_TOPK, ns))) & causal
    bias_t = jnp.where(sel, 0.0, NEG)
    parts = [jnp.zeros((hd, tq), jnp.float32), bias_t]
    if LANES - hd - ns > 0:
        parts.append(jnp.zeros((LANES - hd - ns, tq), jnp.float32))
    selb_ref[...] = jnp.concatenate(parts, axis=0).T.astype(jnp.bfloat16)


def _cmp_select(qn, kct, vc, ng, ovt, B, S):
    T = B * S
    tq = CMP_TILE
    nq = S // tq
    nc = kct.shape[-1]
    ns = S // SLC_BLOCK
    G = NSA_KV_GROUPS
    bf = jnp.bfloat16
    return pl.pallas_call(
        functools.partial(_cmp_select_kernel, tq=tq, nc=nc, ns=ns),
        grid=(B, G, nq),
        in_specs=[
            pl.BlockSpec((tq, NSA_REP * LANES), lambda b, g, i: (b * nq + i, g)),
            pl.BlockSpec((1, 1, LANES, nc), lambda b, g, i: (b, g, 0, 0)),
            pl.BlockSpec((1, 1, nc, LANES), lambda b, g, i: (b, G + g, 0, 0)),
            pl.BlockSpec((tq, LANES), lambda b, g, i: (b * nq + i, g)),
            pl.BlockSpec((ns, nc), lambda b, g, i: (0, 0)),
        ],
        out_specs=[pl.BlockSpec((tq, NSA_REP * NSA_HEAD_DIM), lambda b, g, i: (b * nq + i, g)),
                   pl.BlockSpec((tq, LANES), lambda b, g, i: (b * nq + i, g))],
        out_shape=[jax.ShapeDtypeStruct((T, NSA_HEADS * NSA_HEAD_DIM), bf),
                   jax.ShapeDtypeStruct((T, G * LANES), bf)],
        scratch_shapes=[pltpu.VMEM((ns, tq), jnp.float32)],
        compiler_params=pltpu.CompilerParams(dimension_semantics=("arbitrary",) * 3,
                                             vmem_limit_bytes=VMEM_LIMIT_BYTES),
        name="cmp_select",
    )(qn, kct, vc, ng, ovt)


def _flash_kernel(*refs, hpk, kv_per_step, tq, tk, window, has_sel, has_gate):
    refs = list(refs)
    q_ref = refs.pop(0)
    selb_ref = refs.pop(0) if has_sel else None
    kt_ref = refs.pop(0)
    v_ref = refs.pop(0)
    gate_ref = refs.pop(0) if has_gate else None
    o_ref, q_scr, p_scr, m_scr, alpha_scr, acc_scr = refs
    nh = hpk * kv_per_step
    hd = NSA_HEAD_DIM
    qi = pl.program_id(2)

    for h in range(nh):
        q = q_ref[:, h * LANES:(h + 1) * LANES]
        if has_sel:
            q = q + selb_ref[...]
        q_scr[h] = q
    m_scr[...] = jnp.full(m_scr.shape, NEG, jnp.float32)
    acc_scr[...] = jnp.zeros(acc_scr.shape, jnp.float32)

    def tile_mask(kt):
        qpos = qi * tq + lax.broadcasted_iota(jnp.int32, (tq, tk), 0)
        kpos = kt * tk + lax.broadcasted_iota(jnp.int32, (tq, tk), 1)
        mask = kpos <= qpos
        if window is not None:
            mask = mask & ((qpos - kpos) < window)
        return mask

    def scores(kt, mask):
        for p in range(kv_per_step):
            ktile = kt_ref[0, p, kt]
            for r in range(hpk):
                h = p * hpk + r
                s = _dot(q_scr[h], ktile)
                if mask is not None:
                    s = jnp.where(mask, s, NEG)
                m_prev = m_scr[h]
                m_new = jnp.maximum(m_prev, jnp.max(s, axis=1, keepdims=True))
                alpha_scr[h] = jnp.exp(m_prev - m_new)
                e = jnp.exp(s - jnp.tile(m_new, (1, tk // LANES)))
                if mask is not None:
                    e = jnp.where(mask, e, 0.0)
                p_scr[h] = e.astype(jnp.bfloat16)
                m_scr[h] = m_new

    def values(kt):
        for p in range(kv_per_step):
            vtile = v_ref[pl.ds(pl.multiple_of(kt * tk, tk), tk), p * LANES:(p + 1) * LANES]
            for r in range(hpk):
                h = p * hpk + r
                acc_scr[h] = alpha_scr[h] * acc_scr[h] + _dot(p_scr[h], vtile)

    scores(qi, tile_mask(qi))
    lo = 0 if window is None else jnp.maximum(qi - window // tk, 0)

    def body(kt, carry):
        values(jnp.where(kt == lo, qi, kt - 1))
        scores(kt, None if window is None else tile_mask(kt))
        return carry

    lax.fori_loop(lo, qi, body, 0)
    values(jnp.where(qi == lo, qi, qi - 1))

    lane = lax.broadcasted_iota(jnp.int32, (tq, LANES), 1)
    outs = []
    for h in range(nh):
        acc = acc_scr[h]
        o = acc / jnp.maximum(acc[:, hd:hd + 1], TINY)
        if has_gate:
            o = o * gate_ref[:, h:h + 1]
        outs.append(jnp.where(lane < hd, o, 0.0))
    for i in range(nh // 2):
        pair = outs[2 * i] + pltpu.roll(outs[2 * i + 1], hd, 1)
        o_ref[:, i * LANES:(i + 1) * LANES] = pair.astype(jnp.bfloat16)


def _flash(q, kt, v, B, S, *, hpk, kv_per_step, window=None, selb=None, gate=None, gate_col=0, name):
    T = B * S
    tq = tk = FLASH_TILE
    nq = S // tq
    nkv = kt.shape[1]
    nh = hpk * kv_per_step
    steps = nkv // kv_per_step
    hd = NSA_HEAD_DIM
    in_specs = [pl.BlockSpec((tq, nh * LANES), lambda b, j, i: (b * nq + i, j))]
    args = [q]
    if selb is not None:
        in_specs.append(pl.BlockSpec((tq, LANES), lambda b, j, i: (b * nq + i, j)))
        args.append(selb)
    in_specs.append(pl.BlockSpec((1, kv_per_step, S // tk, LANES, tk), lambda b, j, i: (b, j, 0, 0, 0)))
    in_specs.append(pl.BlockSpec((S, kv_per_step * LANES), lambda b, j, i: (b, j)))
    args += [kt, v]
    if gate is not None:
        in_specs.append(pl.BlockSpec((tq, LANES), lambda b, j, i: (b * nq + i, gate_col + j)))
        args.append(gate)
    return pl.pallas_call(
        functools.partial(_flash_kernel, hpk=hpk, kv_per_step=kv_per_step, tq=tq, tk=tk, window=window,
                          has_sel=selb is not None, has_gate=gate is not None),
        grid=(B, steps, nq),
        in_specs=in_specs,
        out_specs=pl.BlockSpec((tq, nh * hd), lambda b, j, i: (b * nq + i, j)),
        out_shape=jax.ShapeDtypeStruct((T, nkv * hpk * hd), jnp.bfloat16),
        scratch_shapes=[pltpu.VMEM((nh, tq, LANES), jnp.bfloat16),
                        pltpu.VMEM((nh, tq, tk), jnp.bfloat16),
                        pltpu.VMEM((nh, tq, LANES), jnp.float32),
                        pltpu.VMEM((nh, tq, LANES), jnp.float32),
                        pltpu.VMEM((nh, tq, LANES), jnp.float32)],
        compiler_params=pltpu.CompilerParams(dimension_semantics=("arbitrary",) * 3,
                                             vmem_limit_bytes=VMEM_LIMIT_BYTES),
        name=name,
    )(*args)


def _merge_kernel(oc_ref, os_ref, ow_ref, ob_ref, ga_ref, gb_ref, x_ref, wa_ref, wb_ref, wo_ref, out_ref):
    f32 = jnp.float32
    o_a = (oc_ref[...].astype(f32) + os_ref[...].astype(f32) + ow_ref[...].astype(f32)).astype(jnp.bfloat16)
    merged = (ga_ref[...].astype(f32) * _dot(o_a, wa_ref[...])
              + gb_ref[...].astype(f32) * _dot(ob_ref[...], wb_ref[...]))
    out_ref[...] = x_ref[...] + _dot(merged.astype(jnp.bfloat16), wo_ref[...])


def _merge(o_cmp, o_slc, o_win, o_b, ga, gb, x2, lw):
    T = x2.shape[0]
    tm = TOKEN_TILE

    def rows(width):
        return pl.BlockSpec((tm, width), lambda i: (i, 0))

    def full(a):
        return pl.BlockSpec(a.shape, lambda i: (0, 0))

    w = NSA_HEADS * NSA_HEAD_DIM
    return pl.pallas_call(
        _merge_kernel,
        grid=(T // tm,),
        in_specs=[rows(w), rows(w), rows(w), rows(MLA_HEADS * MLA_V), rows(D_MODEL), rows(D_MODEL),
                  rows(D_MODEL), full(lw['nsa_w_o']), full(lw['mla_w_o']), full(lw['w_out'])],
        out_specs=rows(D_MODEL),
        out_shape=jax.ShapeDtypeStruct((T, D_MODEL), jnp.float32),
        compiler_params=pltpu.CompilerParams(dimension_semantics=("arbitrary",),
                                             vmem_limit_bytes=VMEM_LIMIT_BYTES),
        name="merge",
    )(o_cmp, o_slc, o_win, o_b, ga, gb, x2, lw['nsa_w_o'], lw['mla_w_o'], lw['w_out'])


def _mlp_kernel(x_ref, g_ref, wu_ref, wd_ref, gf_ref, out_ref, *, ff_chunk, final):
    x = x_ref[...]
    hn = _rms(x, g_ref[...]).astype(jnp.bfloat16)
    acc = jnp.zeros(x.shape, jnp.float32)
    for c in range(D_FF // ff_chunk):
        h = jnp.maximum(_dot(hn, wu_ref[:, c * ff_chunk:(c + 1) * ff_chunk]), 0.0)
        acc = acc + _dot((h * h).astype(jnp.bfloat16), wd_ref[c * ff_chunk:(c + 1) * ff_chunk, :])
    y = x + acc
    if final:
        y = _rms(y, gf_ref[...])
    out_ref[...] = y


def _mlp(x2, lw, final_norm, final):
    T = x2.shape[0]
    tm = TOKEN_TILE

    def full(a):
        return pl.BlockSpec(a.shape, lambda i: (0, 0))

    return pl.pallas_call(
        functools.partial(_mlp_kernel, ff_chunk=1024, final=final),
        grid=(T // tm,),
        in_specs=[pl.BlockSpec((tm, D_MODEL), lambda i: (i, 0)), full(lw['mlp_norm']),
                  full(lw['w_up']), full(lw['w_down']), full(final_norm)],
        out_specs=pl.BlockSpec((tm, D_MODEL), lambda i: (i, 0)),
        out_shape=jax.ShapeDtypeStruct((T, D_MODEL), jnp.float32),
        compiler_params=pltpu.CompilerParams(dimension_semantics=("arbitrary",),
                                             vmem_limit_bytes=VMEM_LIMIT_BYTES),
        name="mlp",
    )(x2, lw['mlp_norm'], lw['w_up'], lw['w_down'], final_norm)


def _pad_cols(w, width):
    return jnp.pad(w, ((0, 0), (0, width - w.shape[1])))


def _prep_layer(p, l):
    bf = jnp.bfloat16
    hd = NSA_HEAD_DIM
    G = NSA_KV_GROUPS
    w = p['w_in'][l]
    cols = []
    for h in range(NSA_HEADS):
        cols.append(_pad_cols(w[:, _OFF_Q + h * hd:_OFF_Q + (h + 1) * hd], LANES))
    cols.append(w[:, _OFF_KVC:_OFF_KVS])
    for off in (_OFF_KVS, _OFF_KVW):
        for g in range(G):
            cols.append(_pad_cols(w[:, off + (G + g) * hd:off + (G + g + 1) * hd], LANES))
    for br in range(3):
        for g in range(G):
            c0 = _OFF_NG + br * NSA_HEADS + g * NSA_REP
            cols.append(_pad_cols(w[:, c0:c0 + NSA_REP], LANES))
    cols.append(w[:, _OFF_CQ:_OFF_CKV])
    cols.append(w[:, _OFF_CKV:_OFF_KR])
    cols.append(w[:, _OFF_GA:_OFF_GB])
    cols.append(w[:, _OFF_GB:_OFF_GB + D_MODEL])
    w_row = jnp.concatenate(cols, axis=1).astype(bf)
    t_cols = [w[:, off + g * hd:off + (g + 1) * hd] for off in (_OFF_KVS, _OFF_KVW) for g in range(G)]
    t_cols.append(w[:, _OFF_KR:_OFF_GA])
    w_t = jnp.concatenate(t_cols, axis=1).T.astype(bf)

    dq = MLA_NOPE + MLA_ROPE
    wuq = p['mla_w_uq'][l]
    w_uq = jnp.concatenate([_pad_cols(wuq[:, h * dq:(h + 1) * dq], LANES) for h in range(MLA_HEADS)], axis=1)
    wukv = p['mla_w_ukv'][l]
    dkv = MLA_NOPE + MLA_V
    w_v = jnp.concatenate([_pad_cols(wukv[:, h * dkv + MLA_NOPE:(h + 1) * dkv], LANES)
                           for h in range(MLA_HEADS)], axis=1)
    w_kt = jnp.concatenate([wukv[:, h * dkv:h * dkv + MLA_NOPE] for h in range(MLA_HEADS)], axis=1).T

    cmp_w2 = p['cmp_w2'][l]
    cmp_w2p = jnp.pad(cmp_w2, ((0, 0), (0, 0), (0, LANES - hd)))
    return {
        'attn_norm': p['attn_norm'][l][None, :],
        'w_row': w_row, 'w_t': w_t,
        'q_norm': p['mla_q_norm'][l][None, :], 'kv_norm': p['mla_kv_norm'][l][None, :],
        'w_uq': w_uq.astype(bf), 'w_v': w_v.astype(bf), 'w_kt': w_kt.astype(bf),
        'cmp_pe': p['cmp_pe'][l].reshape(2, 1, CMP_BLOCK * hd),
        'cmp_w1': p['cmp_w1'][l].astype(bf),
        'cmp_w2': cmp_w2p.astype(bf),
        'cmp_w2t': jnp.swapaxes(cmp_w2p, 1, 2).astype(bf),
        'nsa_w_o': p['nsa_w_o'][l].astype(bf), 'mla_w_o': p['mla_w_o'][l].astype(bf),
        'w_out': p['w_out'][l].astype(bf),
        'mlp_norm': p['mlp_norm'][l][None, :],
        'w_up': p['w_up'][l].astype(bf), 'w_down': p['w_down'][l].astype(bf),
    }


def _tables(S):
    pos = jnp.arange(S, dtype=jnp.int32).astype(jnp.float32)
    hd = NSA_HEAD_DIM

    def angles(d):
        inv = ROPE_THETA ** (-jnp.arange(0, d, 2, dtype=jnp.float32) / d)
        ang = pos[:, None] * inv[None, :]
        return jnp.cos(ang), jnp.sin(ang)

    cos, sin = angles(hd)
    z = jnp.zeros_like(cos)
    scale = hd ** -0.5
    pad = jnp.zeros((S, LANES - hd), jnp.float32)
    cq = jnp.concatenate([cos, cos, pad], axis=1) * scale
    s1q = jnp.concatenate([z, sin, pad], axis=1) * scale
    s2q = jnp.concatenate([-sin, z, pad], axis=1) * scale
    cc = jnp.concatenate([cos, cos, cos, cos], axis=1)
    s1c = jnp.concatenate([z, sin, z, sin], axis=1)
    s2c = jnp.concatenate([-sin, z, -sin, z], axis=1)

    cos16, sin16 = angles(MLA_ROPE)
    z16 = jnp.zeros_like(cos16)
    sm = (MLA_NOPE + MLA_ROPE) ** -0.5
    one = jnp.ones((S, MLA_NOPE), jnp.float32)
    z64 = jnp.zeros((S, MLA_NOPE), jnp.float32)
    pad32 = jnp.zeros((S, LANES - MLA_NOPE - MLA_ROPE), jnp.float32)
    cm = jnp.concatenate([one, cos16, cos16, pad32], axis=1) * sm
    s1m = jnp.concatenate([z64, z16, sin16, pad32], axis=1) * sm
    s2m = jnp.concatenate([z64, -sin16, z16, pad32], axis=1) * sm
    return {'cq': cq, 's1q': s1q, 's2q': s2q, 'cc': cc, 's1c': s1c, 's2c': s2c,
            'cos_t': cos.T, 'sin_t': sin.T, 'cm': cm, 's1m': s1m, 's2m': s2m,
            'cos_t16': cos16.T, 'sin_t16': sin16.T}


def _overlap_t(S):
    nc = S // CMP_STRIDE
    n_cmp = (S - CMP_BLOCK) // CMP_STRIDE + 1
    ns = S // SLC_BLOCK
    start = np.arange(nc) * CMP_STRIDE
    sel = np.arange(ns) * SLC_BLOCK
    lo = np.maximum(start[None, :], sel[:, None])
    hi = np.minimum(start[None, :] + CMP_BLOCK, sel[:, None] + SLC_BLOCK)
    ov = np.clip(hi - lo, 0, None) / CMP_BLOCK
    ov[:, n_cmp:] = 0.0
    return jnp.asarray(ov, jnp.bfloat16)


def _cmp_blocks(cmp_in, B, S):
    hd = NSA_HEAD_DIM
    nc = S // CMP_STRIDE
    r = cmp_in.reshape(B, nc, CMP_STRIDE, 2 * NSA_KV_GROUPS, hd)
    r = r.transpose(0, 3, 1, 2, 4).reshape(B, 2 * NSA_KV_GROUPS, nc, CMP_STRIDE * hd)
    nxt = jnp.concatenate([r[:, :, 1:], jnp.zeros_like(r[:, :, :1])], axis=2)
    return jnp.concatenate([r, nxt], axis=-1)


def _forward(x, params, depth):
    B, S, D = x.shape
    T = B * S
    tabs = _tables(S)
    ovt = _overlap_t(S)
    x2 = x.reshape(T, D)
    final_norm = params['final_norm'][None, :]
    G = NSA_KV_GROUPS
    for l in range(depth):
        lw = _prep_layer(params, l)
        (qn, cmp_in, vs, vw, ng, ga, gb, kts, ktw, qm, vm, ktm) = _in_proj(x2, lw, tabs, B, S)
        kct, vc = _compress(_cmp_blocks(cmp_in, B, S), lw)
        o_cmp, selb = _cmp_select(qn, kct, vc, ng, ovt, B, S)
        o_slc = _flash(qn, kts, vs, B, S, hpk=NSA_REP, kv_per_step=1, selb=selb,
                       gate=ng, gate_col=G, name="flash_slc")
        o_win = _flash(qn, ktw, vw, B, S, hpk=NSA_REP, kv_per_step=1, window=WINDOW,
                       gate=ng, gate_col=2 * G, name="flash_win")
        o_b = _flash(qm, ktm, vm, B, S, hpk=1, kv_per_step=4, name="flash_mla")
        x2 = _merge(o_cmp, o_slc, o_win, o_b, ga, gb, x2, lw)
        x2 = _mlp(x2, lw, final_norm, final=(l == depth - 1))
    return x2.reshape(B, S, D)


def kernel(x, attn_norm, w_in, cmp_pe, cmp_w1, cmp_w2, nsa_w_o, mla_q_norm, mla_kv_norm, mla_w_uq,
           mla_w_ukv, mla_w_o, w_out, mlp_norm, w_up, w_down, final_norm):
    params = dict(attn_norm=attn_norm, w_in=w_in, cmp_pe=cmp_pe, cmp_w1=cmp_w1, cmp_w2=cmp_w2,
                  nsa_w_o=nsa_w_o, mla_q_norm=mla_q_norm, mla_kv_norm=mla_kv_norm, mla_w_uq=mla_w_uq,
                  mla_w_ukv=mla_w_ukv, mla_w_o=mla_w_o, w_out=w_out, mlp_norm=mlp_norm, w_up=w_up,
                  w_down=w_down, final_norm=final_norm)
    return _forward(x, params, w_in.shape[0])
```

```python
import functools

import numpy as np
import jax
import jax.numpy as jnp
from jax import lax
from jax.experimental import pallas as pl
from jax.experimental.pallas import tpu as pltpu

EPS = 1e-6
ROPE_THETA = 10000.0
NEG = -1e30
TINY = 1e-30
LOG2E = 1.4426950408889634

D_MODEL = 1024
NSA_HEADS = 8
NSA_KV_GROUPS = 2
NSA_REP = NSA_HEADS // NSA_KV_GROUPS
NSA_HEAD_DIM = 64
CMP_BLOCK = 32
CMP_STRIDE = 16
CMP_HIDDEN = 4 * NSA_HEAD_DIM
SLC_BLOCK = 64
SLC_TOPK = 16
N_LOCAL_FORCED = 2
FORCE_BONUS = 1e4
WINDOW = 512

MLA_HEADS = 8
MLA_Q_LORA = 256
MLA_KV_LORA = 128
MLA_NOPE = 64
MLA_ROPE = 32
MLA_V = 64
D_FF = 4 * D_MODEL

LANES = 128
SUBLANES = 8
VMEM_LIMIT_BYTES = 56 * 1024 * 1024

TOKEN_TILE = 512
FLASH_TILE = 512
CMP_TILE = 256

_OFF_Q = 0
_OFF_KVC = _OFF_Q + NSA_HEADS * NSA_HEAD_DIM
_OFF_KVS = _OFF_KVC + 2 * NSA_KV_GROUPS * NSA_HEAD_DIM
_OFF_KVW = _OFF_KVS + 2 * NSA_KV_GROUPS * NSA_HEAD_DIM
_OFF_NG = _OFF_KVW + 2 * NSA_KV_GROUPS * NSA_HEAD_DIM
_OFF_CQ = _OFF_NG + 3 * NSA_HEADS
_OFF_CKV = _OFF_CQ + MLA_Q_LORA
_OFF_KR = _OFF_CKV + MLA_KV_LORA
_OFF_GA = _OFF_KR + MLA_ROPE
_OFF_GB = _OFF_GA + D_MODEL

_R_Q = 0
_R_CMP = _R_Q + NSA_HEADS * LANES
_R_VS = _R_CMP + 256
_R_VW = _R_VS + NSA_KV_GROUPS * LANES
_R_NG = _R_VW + NSA_KV_GROUPS * LANES
_R_CQ = _R_NG + 3 * NSA_KV_GROUPS * LANES
_R_CKV = _R_CQ + MLA_Q_LORA
_R_GA = _R_CKV + MLA_KV_LORA
_R_GB = _R_GA + D_MODEL
_R_END = _R_GB + D_MODEL


def _nt_dot(a, b):
    return lax.dot_general(a, b, (((1,), (1,)), ((), ())), preferred_element_type=jnp.float32)


def _dot(a, b):
    return jnp.dot(a, b, preferred_element_type=jnp.float32)


def _rms(xf, g):
    return xf * lax.rsqrt(jnp.mean(xf * xf, axis=-1, keepdims=True) + EPS) * g


def _rope_rows(y, c, s1, s2, sh):
    return y * c + pltpu.roll(y, sh, 1) * s1 + pltpu.roll(y, LANES - sh, 1) * s2


def _rope_cols(y, cos_t, sin_t):
    h = y.shape[0] // 2
    a, b = y[:h], y[h:]
    return jnp.concatenate([a * cos_t - b * sin_t, a * sin_t + b * cos_t], axis=0)


def _in_proj_kernel(x_ref, g_ref, w_ref, wt_ref, qng_ref, kvng_ref, wuq_ref, wv_ref, wkt_ref,
                    cq_ref, s1q_ref, s2q_ref, cc_ref, s1c_ref, s2c_ref, cost_ref, sint_ref,
                    cm_ref, s1m_ref, s2m_ref, cost16_ref, sint16_ref,
                    qn_ref, cmp_ref, vs_ref, vw_ref, ng_ref, ga_ref, gb_ref, kts_ref, ktw_ref,
                    qm_ref, vm_ref, ktm_ref, *, tm, tk, s_tiles):
    si = pl.program_id(0) % s_tiles
    hd = NSA_HEAD_DIM
    xn = _rms(x_ref[...], g_ref[...]).astype(jnp.bfloat16)

    lane = lax.broadcasted_iota(jnp.int32, (tm, LANES), 1)
    ones_col = jnp.where(lane == hd, 1.0, 0.0)

    y = _dot(xn, w_ref[:, _R_Q:_R_CMP])
    cq, s1q, s2q = cq_ref[...], s1q_ref[...], s2q_ref[...]
    for h in range(NSA_HEADS):
        blk = y[:, h * LANES:(h + 1) * LANES]
        qn_ref[:, h * LANES:(h + 1) * LANES] = _rope_rows(blk, cq, s1q, s2q, hd // 2).astype(jnp.bfloat16)

    y = _dot(xn, w_ref[:, _R_CMP:_R_VS])
    cmp_ref[:, :LANES] = _rope_rows(y[:, :LANES], cc_ref[...], s1c_ref[...], s2c_ref[...], hd // 2)
    cmp_ref[:, LANES:] = y[:, LANES:]

    y = _dot(xn, w_ref[:, _R_VS:_R_VW])
    for g in range(NSA_KV_GROUPS):
        vs_ref[:, g * LANES:(g + 1) * LANES] = (y[:, g * LANES:(g + 1) * LANES] + ones_col).astype(jnp.bfloat16)
    y = _dot(xn, w_ref[:, _R_VW:_R_NG])
    for g in range(NSA_KV_GROUPS):
        vw_ref[:, g * LANES:(g + 1) * LANES] = (y[:, g * LANES:(g + 1) * LANES] + ones_col).astype(jnp.bfloat16)

    ng_ref[...] = jax.nn.sigmoid(_dot(xn, w_ref[:, _R_NG:_R_CQ]))
    ga_ref[...] = jax.nn.sigmoid(_dot(xn, w_ref[:, _R_GA:_R_GB])).astype(jnp.bfloat16)
    gb_ref[...] = jax.nn.sigmoid(_dot(xn, w_ref[:, _R_GB:_R_END])).astype(jnp.bfloat16)

    cqn = _rms(_dot(xn, w_ref[:, _R_CQ:_R_CKV]), qng_ref[...]).astype(jnp.bfloat16)
    y = _dot(cqn, wuq_ref[...])
    cm, s1m, s2m = cm_ref[...], s1m_ref[...], s2m_ref[...]
    for h in range(MLA_HEADS):
        blk = y[:, h * LANES:(h + 1) * LANES]
        qm_ref[:, h * LANES:(h + 1) * LANES] = _rope_rows(blk, cm, s1m, s2m, MLA_ROPE // 2).astype(jnp.bfloat16)

    ckvn = _rms(_dot(xn, w_ref[:, _R_CKV:_R_GA]), kvng_ref[...]).astype(jnp.bfloat16)
    y = _dot(ckvn, wv_ref[...])
    for h in range(MLA_HEADS):
        vm_ref[:, h * LANES:(h + 1) * LANES] = (y[:, h * LANES:(h + 1) * LANES] + ones_col).astype(jnp.bfloat16)
    knt = _nt_dot(wkt_ref[...], ckvn)

    yt = _nt_dot(wt_ref[...], xn)
    cos_t, sin_t = cost_ref[...], sint_ref[...]
    kpos = si * tm + lax.broadcasted_iota(jnp.int32, (SLC_BLOCK, tm), 1)
    blk_id = lax.broadcasted_iota(jnp.int32, (SLC_BLOCK, tm), 0)
    sel_rows = jnp.where((kpos // SLC_BLOCK) == blk_id, 1.0, 0.0)
    zero_rows = jnp.zeros((LANES - hd, tm), jnp.float32)
    for g in range(NSA_KV_GROUPS):
        ks = _rope_cols(yt[g * hd:(g + 1) * hd], cos_t, sin_t)
        kw = _rope_cols(yt[(NSA_KV_GROUPS + g) * hd:(NSA_KV_GROUPS + g + 1) * hd], cos_t, sin_t)
        ks = jnp.concatenate([ks, sel_rows], axis=0).astype(jnp.bfloat16)
        kw = jnp.concatenate([kw, zero_rows], axis=0).astype(jnp.bfloat16)
        for c in range(tm // tk):
            kts_ref[0, g, c] = ks[:, c * tk:(c + 1) * tk]
            ktw_ref[0, g, c] = kw[:, c * tk:(c + 1) * tk]
    kpe = _rope_cols(yt[2 * NSA_KV_GROUPS * hd:], cost16_ref[...], sint16_ref[...])
    pad_rows = jnp.zeros((LANES - MLA_NOPE - MLA_ROPE, tm), jnp.float32)
    for h in range(MLA_HEADS):
        km = jnp.concatenate([knt[h * MLA_NOPE:(h + 1) * MLA_NOPE], kpe, pad_rows], axis=0).astype(jnp.bfloat16)
        for c in range(tm // tk):
            ktm_ref[0, h, c] = km[:, c * tk:(c + 1) * tk]


def _in_proj(x2, lw, tabs, B, S):
    T = B * S
    tm, tk = TOKEN_TILE, FLASH_TILE
    s_tiles = S // tm
    nkt = S // tk
    bf = jnp.bfloat16

    def full(a):
        return pl.BlockSpec(a.shape, lambda i, _n=a.ndim: (0,) * _n)

    def rows(width):
        return pl.BlockSpec((tm, width), lambda i: (i, 0))

    def tab_rows(a):
        return pl.BlockSpec((tm, a.shape[1]), lambda i: (i % s_tiles, 0))

    def tab_cols(a):
        return pl.BlockSpec((a.shape[0], tm), lambda i: (0, i % s_tiles))

    def kt_spec(n):
        return pl.BlockSpec((1, n, tm // tk, LANES, tk), lambda i: (i // s_tiles, 0, i % s_tiles, 0, 0))

    weights = [lw['attn_norm'], lw['w_row'], lw['w_t'], lw['q_norm'], lw['kv_norm'],
               lw['w_uq'], lw['w_v'], lw['w_kt']]
    row_tabs = [tabs['cq'], tabs['s1q'], tabs['s2q'], tabs['cc'], tabs['s1c'], tabs['s2c']]
    mla_tabs = [tabs['cm'], tabs['s1m'], tabs['s2m']]
    in_specs = ([rows(D_MODEL)] + [full(w) for w in weights] + [tab_rows(t) for t in row_tabs]
                + [tab_cols(tabs['cos_t']), tab_cols(tabs['sin_t'])]
                + [tab_rows(t) for t in mla_tabs]
                + [tab_cols(tabs['cos_t16']), tab_cols(tabs['sin_t16'])])
    out_shape = [
        jax.ShapeDtypeStruct((T, NSA_HEADS * LANES), bf),
        jax.ShapeDtypeStruct((T, 256), jnp.float32),
        jax.ShapeDtypeStruct((T, NSA_KV_GROUPS * LANES), bf),
        jax.ShapeDtypeStruct((T, NSA_KV_GROUPS * LANES), bf),
        jax.ShapeDtypeStruct((T, 3 * NSA_KV_GROUPS * LANES), jnp.float32),
        jax.ShapeDtypeStruct((T, D_MODEL), bf),
        jax.ShapeDtypeStruct((T, D_MODEL), bf),
        jax.ShapeDtypeStruct((B, NSA_KV_GROUPS, nkt, LANES, tk), bf),
        jax.ShapeDtypeStruct((B, NSA_KV_GROUPS, nkt, LANES, tk), bf),
        jax.ShapeDtypeStruct((T, MLA_HEADS * LANES), bf),
        jax.ShapeDtypeStruct((T, MLA_HEADS * LANES), bf),
        jax.ShapeDtypeStruct((B, MLA_HEADS, nkt, LANES, tk), bf),
    ]
    out_specs = [rows(NSA_HEADS * LANES), rows(256), rows(NSA_KV_GROUPS * LANES), rows(NSA_KV_GROUPS * LANES),
                 rows(3 * NSA_KV_GROUPS * LANES), rows(D_MODEL), rows(D_MODEL),
                 kt_spec(NSA_KV_GROUPS), kt_spec(NSA_KV_GROUPS),
                 rows(MLA_HEADS * LANES), rows(MLA_HEADS * LANES), kt_spec(MLA_HEADS)]
    return pl.pallas_call(
        functools.partial(_in_proj_kernel, tm=tm, tk=tk, s_tiles=s_tiles),
        grid=(T // tm,),
        in_specs=in_specs,
        out_specs=out_specs,
        out_shape=out_shape,
        compiler_params=pltpu.CompilerParams(dimension_semantics=("arbitrary",),
                                             vmem_limit_bytes=VMEM_LIMIT_BYTES),
        name="in_proj",
    )(x2, *weights, *row_tabs, tabs['cos_t'], tabs['sin_t'], *mla_tabs, tabs['cos_t16'], tabs['sin_t16'])


def _compress_kernel(blk_ref, pe_ref, w1_ref, w2_ref, w2t_ref, ot_ref, or_ref):
    tb = (blk_ref[0, 0] + pe_ref[0]).astype(jnp.bfloat16)
    h = _dot(tb, w1_ref[0])
    h = (h * jax.nn.sigmoid(h)).astype(jnp.bfloat16)
    ot_ref[0, 0] = _nt_dot(w2t_ref[0], h).astype(jnp.bfloat16)
    or_ref[0, 0] = _dot(h, w2_ref[0]).astype(jnp.bfloat16)


def _compress(blocks, lw):
    B, _, NC, K = blocks.shape
    bf = jnp.bfloat16
    return pl.pallas_call(
        _compress_kernel,
        grid=(B, 2 * NSA_KV_GROUPS),
        in_specs=[
            pl.BlockSpec((1, 1, NC, K), lambda b, c: (b, c, 0, 0)),
            pl.BlockSpec((1, 1, K), lambda b, c: (c // NSA_KV_GROUPS, 0, 0)),
            pl.BlockSpec((1, K, CMP_HIDDEN), lambda b, c: (c // NSA_KV_GROUPS, 0, 0)),
            pl.BlockSpec((1, CMP_HIDDEN, LANES), lambda b, c: (c // NSA_KV_GROUPS, 0, 0)),
            pl.BlockSpec((1, LANES, CMP_HIDDEN), lambda b, c: (c // NSA_KV_GROUPS, 0, 0)),
        ],
        out_specs=[pl.BlockSpec((1, 1, LANES, NC), lambda b, c: (b, c, 0, 0)),
                   pl.BlockSpec((1, 1, NC, LANES), lambda b, c: (b, c, 0, 0))],
        out_shape=[jax.ShapeDtypeStruct((B, 2 * NSA_KV_GROUPS, LANES, NC), bf),
                   jax.ShapeDtypeStruct((B, 2 * NSA_KV_GROUPS, NC, LANES), bf)],
        compiler_params=pltpu.CompilerParams(dimension_semantics=("arbitrary", "arbitrary"),
                                             vmem_limit_bytes=VMEM_LIMIT_BYTES),
        name="compress",
    )(blocks, lw['cmp_pe'], lw['cmp_w1'], lw['cmp_w2'], lw['cmp_w2t'])


def _cmp_select_kernel(q_ref, kct_ref, vc_ref, gate_ref, ovt_ref, o_ref, selb_ref, score_scr, *, tq, nc, ns):
    qi = pl.program_id(2)
    hd = NSA_HEAD_DIM
    pos = qi * tq + lax.broadcasted_iota(jnp.int32, (tq, nc), 0)
    col = lax.broadcasted_iota(jnp.int32, (tq, nc), 1)
    valid = (col * CMP_STRIDE + (CMP_BLOCK - 1)) <= pos
    lane = lax.broadcasted_iota(jnp.int32, (tq, LANES), 1)
    kct = kct_ref[0, 0]
    vc = vc_ref[0, 0]
    gate = gate_ref[...]

    psum = jnp.zeros((tq, nc), jnp.float32)
    outs = []
    for r in range(NSA_REP):
        s = _dot(q_ref[:, r * LANES:(r + 1) * LANES], kct)
        sf = jnp.where(valid, s, NEG)
        m = jnp.max(sf, axis=1, keepdims=True)
        e = jnp.where(valid, jnp.exp2(sf - m), 0.0)
        p = e / jnp.maximum(jnp.sum(e, axis=1, keepdims=True), TINY)
        psum = psum + p
        o = _dot(p.astype(jnp.bfloat16), vc) * gate[:, r:r + 1]
        outs.append(jnp.where(lane < hd, o, 0.0))
    for i in range(NSA_REP // 2):
        pair = outs[2 * i] + pltpu.roll(outs[2 * i + 1], hd, 1)
        o_ref[:, i * LANES:(i + 1) * LANES] = pair.astype(jnp.bfloat16)

    hi = psum.astype(jnp.bfloat16)
    lo = (psum - hi.astype(jnp.float32)).astype(jnp.bfloat16)
    ovt = ovt_ref[...]
    imp = _nt_dot(ovt, hi) + _nt_dot(ovt, lo)
    post = qi * tq + lax.broadcasted_iota(jnp.int32, (ns, tq), 1)
    blk = lax.broadcasted_iota(jnp.int32, (ns, tq), 0)
    cur = post // SLC_BLOCK
    causal = blk <= cur
    forced = (blk == 0) | ((cur - blk >= 0) & (cur - blk < N_LOCAL_FORCED))
    score = jnp.where(causal, imp + jnp.where(forced, FORCE_BONUS, 0.0), NEG)
    score_scr[...] = score

    sub_id = lax.broadcasted_iota(jnp.int32, (SUBLANES, tq), 0)
    counts = []
    for rg in range(ns // SUBLANES):
        sub = score[rg * SUBLANES:(rg + 1) * SUBLANES]
        c = jnp.zeros((SUBLANES, tq), jnp.float32)
        for i in range(ns):
            bi = score_scr[i:i + 1, :]
            if i < rg * SUBLANES:
                ahead = bi >= sub
            elif i >= (rg + 1) * SUBLANES:
                ahead = bi > sub
            else:
                ahead = (bi > sub) | ((bi == sub) & (sub_id > (i - rg * SUBLANES)))
            c = c + jnp.where(ahead, 1.0, 0.0)
        counts.append(c)
    cnt = jnp.concatenate(counts, axis=0)
    sel = (cnt < float(min(SLC_TOPK, ns))) & causal
    bias_t = jnp.where(sel, 0.0, NEG)
    parts = [jnp.zeros((hd, tq), jnp.float32), bias_t]
    if LANES - hd - ns > 0:
        parts.append(jnp.zeros((LANES - hd - ns, tq), jnp.float32))
    selb_ref[...] = jnp.concatenate(parts, axis=0).T.astype(jnp.bfloat16)


def _cmp_select(qn, kct, vc, ng, ovt, B, S):
    T = B * S
    tq = CMP_TILE
    nq = S // tq
    nc = kct.shape[-1]
    ns = S // SLC_BLOCK
    G = NSA_KV_GROUPS
    bf = jnp.bfloat16
    return pl.pallas_call(
        functools.partial(_cmp_select_kernel, tq=tq, nc=nc, ns=ns),
        grid=(B, G, nq),
        in_specs=[
            pl.BlockSpec((tq, NSA_REP * LANES), lambda b, g, i: (b * nq + i, g)),
            pl.BlockSpec((1, 1, LANES, nc), lambda b, g, i: (b, g, 0, 0)),
            pl.BlockSpec((1, 1, nc, LANES), lambda b, g, i: (b, G + g, 0, 0)),
            pl.BlockSpec((tq, LANES), lambda b, g, i: (b * nq + i, g)),
            pl.BlockSpec((ns, nc), lambda b, g, i: (0, 0)),
        ],
        out_specs=[pl.BlockSpec((tq, NSA_REP * NSA_HEAD_DIM), lambda b, g, i: (b * nq + i, g)),
                   pl.BlockSpec((tq, LANES), lambda b, g, i: (b * nq + i, g))],
        out_shape=[jax.ShapeDtypeStruct((T, NSA_HEADS * NSA_HEAD_DIM), bf),
                   jax.ShapeDtypeStruct((T, G * LANES), bf)],
        scratch_shapes=[pltpu.VMEM((ns, tq), jnp.float32)],
        compiler_params=pltpu.CompilerParams(dimension_semantics=("arbitrary",) * 3,
                                             vmem_limit_bytes=VMEM_LIMIT_BYTES),
        name="cmp_select",
    )(qn, kct, vc, ng, ovt)


def _flash_kernel(*refs, hpk, kv_per_step, tq, tk, window, has_sel, has_gate):
    refs = list(refs)
    q_ref = refs.pop(0)
    selb_ref = refs.pop(0) if has_sel else None
    kt_ref = refs.pop(0)
    v_ref = refs.pop(0)
    bias_ref = refs.pop(0)
    gate_ref = refs.pop(0) if has_gate else None
    o_ref, q_scr, p_scr, m_scr, alpha_scr, acc_scr = refs
    nh = hpk * kv_per_step
    hd = NSA_HEAD_DIM
    qi = pl.program_id(2)

    for h in range(nh):
        q = q_ref[:, h * LANES:(h + 1) * LANES]
        if has_sel:
            q = q + selb_ref[...]
        q_scr[h] = q
    m_scr[...] = jnp.full(m_scr.shape, NEG, jnp.float32)
    acc_scr[...] = jnp.zeros(acc_scr.shape, jnp.float32)

    def scores(kt, bias_idx):
        for p in range(kv_per_step):
            ktile = kt_ref[0, p, kt]
            for r in range(hpk):
                h = p * hpk + r
                s = _dot(q_scr[h], ktile)
                if bias_idx is not None:
                    s = s + bias_ref[bias_idx]
                m_prev = m_scr[h]
                m_new = jnp.maximum(m_prev, jnp.max(s, axis=1, keepdims=True))
                alpha_scr[h] = jnp.exp2(m_prev - m_new)
                p_scr[h] = jnp.exp2(s - jnp.tile(m_new, (1, tk // LANES))).astype(jnp.bfloat16)
                m_scr[h] = m_new

    def values(kt):
        for p in range(kv_per_step):
            vtile = v_ref[pl.ds(pl.multiple_of(kt * tk, tk), tk), p * LANES:(p + 1) * LANES]
            for r in range(hpk):
                h = p * hpk + r
                acc_scr[h] = alpha_scr[h] * acc_scr[h] + _dot(p_scr[h], vtile)

    scores(qi, 0)
    lo = 0 if window is None else jnp.maximum(qi - 1, 0)

    def body(kt, carry):
        values(jnp.where(kt == lo, qi, kt - 1))
        scores(kt, None if window is None else 1)
        return carry

    lax.fori_loop(lo, qi, body, 0)
    values(jnp.where(qi == lo, qi, qi - 1))

    lane = lax.broadcasted_iota(jnp.int32, (tq, LANES), 1)
    outs = []
    for h in range(nh):
        acc = acc_scr[h]
        o = acc / jnp.maximum(acc[:, hd:hd + 1], TINY)
        if has_gate:
            o = o * gate_ref[:, h:h + 1]
        outs.append(jnp.where(lane < hd, o, 0.0))
    for i in range(nh // 2):
        pair = outs[2 * i] + pltpu.roll(outs[2 * i + 1], hd, 1)
        o_ref[:, i * LANES:(i + 1) * LANES] = pair.astype(jnp.bfloat16)


def _tile_biases(t):
    r = np.arange(t)[:, None]
    c = np.arange(t)[None, :]
    return jnp.asarray(np.stack([np.where(c <= r, 0.0, NEG), np.where(c > r, 0.0, NEG)]), jnp.float32)


def _flash(q, kt, v, B, S, *, hpk, kv_per_step, window=None, selb=None, gate=None, gate_col=0, name):
    T = B * S
    tq = tk = FLASH_TILE
    assert window is None or window == tk
    nq = S // tq
    nkv = kt.shape[1]
    nh = hpk * kv_per_step
    steps = nkv // kv_per_step
    hd = NSA_HEAD_DIM
    in_specs = [pl.BlockSpec((tq, nh * LANES), lambda b, j, i: (b * nq + i, j))]
    args = [q]
    if selb is not None:
        in_specs.append(pl.BlockSpec((tq, LANES), lambda b, j, i: (b * nq + i, j)))
        args.append(selb)
    in_specs.append(pl.BlockSpec((1, kv_per_step, S // tk, LANES, tk), lambda b, j, i: (b, j, 0, 0, 0)))
    in_specs.append(pl.BlockSpec((S, kv_per_step * LANES), lambda b, j, i: (b, j)))
    in_specs.append(pl.BlockSpec((2, tq, tk), lambda b, j, i: (0, 0, 0)))
    args += [kt, v, _tile_biases(tq)]
    if gate is not None:
        in_specs.append(pl.BlockSpec((tq, LANES), lambda b, j, i: (b * nq + i, gate_col + j)))
        args.append(gate)
    return pl.pallas_call(
        functools.partial(_flash_kernel, hpk=hpk, kv_per_step=kv_per_step, tq=tq, tk=tk, window=window,
                          has_sel=selb is not None, has_gate=gate is not None),
        grid=(B, steps, nq),
        in_specs=in_specs,
        out_specs=pl.BlockSpec((tq, nh * hd), lambda b, j, i: (b * nq + i, j)),
        out_shape=jax.ShapeDtypeStruct((T, nkv * hpk * hd), jnp.bfloat16),
        scratch_shapes=[pltpu.VMEM((nh, tq, LANES), jnp.bfloat16),
                        pltpu.VMEM((nh, tq, tk), jnp.bfloat16),
                        pltpu.VMEM((nh, tq, LANES), jnp.float32),
                        pltpu.VMEM((nh, tq, LANES), jnp.float32),
                        pltpu.VMEM((nh, tq, LANES), jnp.float32)],
        compiler_params=pltpu.CompilerParams(dimension_semantics=("arbitrary",) * 3,
                                             vmem_limit_bytes=VMEM_LIMIT_BYTES),
        name=name,
    )(*args)


def _merge_kernel(oc_ref, os_ref, ow_ref, ob_ref, ga_ref, gb_ref, x_ref, wa_ref, wb_ref, wo_ref, out_ref):
    f32 = jnp.float32
    o_a = (oc_ref[...].astype(f32) + os_ref[...].astype(f32) + ow_ref[...].astype(f32)).astype(jnp.bfloat16)
    merged = (ga_ref[...].astype(f32) * _dot(o_a, wa_ref[...])
              + gb_ref[...].astype(f32) * _dot(ob_ref[...], wb_ref[...]))
    out_ref[...] = x_ref[...] + _dot(merged.astype(jnp.bfloat16), wo_ref[...])


def _merge(o_cmp, o_slc, o_win, o_b, ga, gb, x2, lw):
    T = x2.shape[0]
    tm = TOKEN_TILE

    def rows(width):
        return pl.BlockSpec((tm, width), lambda i: (i, 0))

    def full(a):
        return pl.BlockSpec(a.shape, lambda i: (0, 0))

    w = NSA_HEADS * NSA_HEAD_DIM
    return pl.pallas_call(
        _merge_kernel,
        grid=(T // tm,),
        in_specs=[rows(w), rows(w), rows(w), rows(MLA_HEADS * MLA_V), rows(D_MODEL), rows(D_MODEL),
                  rows(D_MODEL), full(lw['nsa_w_o']), full(lw['mla_w_o']), full(lw['w_out'])],
        out_specs=rows(D_MODEL),
        out_shape=jax.ShapeDtypeStruct((T, D_MODEL), jnp.float32),
        compiler_params=pltpu.CompilerParams(dimension_semantics=("arbitrary",),
                                             vmem_limit_bytes=VMEM_LIMIT_BYTES),
        name="merge",
    )(o_cmp, o_slc, o_win, o_b, ga, gb, x2, lw['nsa_w_o'], lw['mla_w_o'], lw['w_out'])


def _mlp_kernel(x_ref, g_ref, wu_ref, wd_ref, gf_ref, out_ref, *, ff_chunk, final):
    x = x_ref[...]
    hn = _rms(x, g_ref[...]).astype(jnp.bfloat16)
    acc = jnp.zeros(x.shape, jnp.float32)
    for c in range(D_FF // ff_chunk):
        h = jnp.maximum(_dot(hn, wu_ref[:, c * ff_chunk:(c + 1) * ff_chunk]), 0.0)
        acc = acc + _dot((h * h).astype(jnp.bfloat16), wd_ref[c * ff_chunk:(c + 1) * ff_chunk, :])
    y = x + acc
    if final:
        y = _rms(y, gf_ref[...])
    out_ref[...] = y


def _mlp(x2, lw, final_norm, final):
    T = x2.shape[0]
    tm = TOKEN_TILE

    def full(a):
        return pl.BlockSpec(a.shape, lambda i: (0, 0))

    return pl.pallas_call(
        functools.partial(_mlp_kernel, ff_chunk=1024, final=final),
        grid=(T // tm,),
        in_specs=[pl.BlockSpec((tm, D_MODEL), lambda i: (i, 0)), full(lw['mlp_norm']),
                  full(lw['w_up']), full(lw['w_down']), full(final_norm)],
        out_specs=pl.BlockSpec((tm, D_MODEL), lambda i: (i, 0)),
        out_shape=jax.ShapeDtypeStruct((T, D_MODEL), jnp.float32),
        compiler_params=pltpu.CompilerParams(dimension_semantics=("arbitrary",),
                                             vmem_limit_bytes=VMEM_LIMIT_BYTES),
        name="mlp",
    )(x2, lw['mlp_norm'], lw['w_up'], lw['w_down'], final_norm)


def _pad_cols(w, width):
    return jnp.pad(w, ((0, 0), (0, width - w.shape[1])))


def _prep_layer(p, l):
    bf = jnp.bfloat16
    hd = NSA_HEAD_DIM
    G = NSA_KV_GROUPS
    w = p['w_in'][l]
    cols = []
    for h in range(NSA_HEADS):
        cols.append(_pad_cols(w[:, _OFF_Q + h * hd:_OFF_Q + (h + 1) * hd], LANES))
    cols.append(w[:, _OFF_KVC:_OFF_KVS])
    for off in (_OFF_KVS, _OFF_KVW):
        for g in range(G):
            cols.append(_pad_cols(w[:, off + (G + g) * hd:off + (G + g + 1) * hd], LANES))
    for br in range(3):
        for g in range(G):
            c0 = _OFF_NG + br * NSA_HEADS + g * NSA_REP
            cols.append(_pad_cols(w[:, c0:c0 + NSA_REP], LANES))
    cols.append(w[:, _OFF_CQ:_OFF_CKV])
    cols.append(w[:, _OFF_CKV:_OFF_KR])
    cols.append(w[:, _OFF_GA:_OFF_GB])
    cols.append(w[:, _OFF_GB:_OFF_GB + D_MODEL])
    w_row = jnp.concatenate(cols, axis=1).astype(bf)
    t_cols = [w[:, off + g * hd:off + (g + 1) * hd] for off in (_OFF_KVS, _OFF_KVW) for g in range(G)]
    t_cols.append(w[:, _OFF_KR:_OFF_GA])
    w_t = jnp.concatenate(t_cols, axis=1).T.astype(bf)

    dq = MLA_NOPE + MLA_ROPE
    wuq = p['mla_w_uq'][l]
    w_uq = jnp.concatenate([_pad_cols(wuq[:, h * dq:(h + 1) * dq], LANES) for h in range(MLA_HEADS)], axis=1)
    wukv = p['mla_w_ukv'][l]
    dkv = MLA_NOPE + MLA_V
    w_v = jnp.concatenate([_pad_cols(wukv[:, h * dkv + MLA_NOPE:(h + 1) * dkv], LANES)
                           for h in range(MLA_HEADS)], axis=1)
    w_kt = jnp.concatenate([wukv[:, h * dkv:h * dkv + MLA_NOPE] for h in range(MLA_HEADS)], axis=1).T

    cmp_w2 = p['cmp_w2'][l]
    cmp_w2p = jnp.pad(cmp_w2, ((0, 0), (0, 0), (0, LANES - hd)))
    return {
        'attn_norm': p['attn_norm'][l][None, :],
        'w_row': w_row, 'w_t': w_t,
        'q_norm': p['mla_q_norm'][l][None, :], 'kv_norm': p['mla_kv_norm'][l][None, :],
        'w_uq': w_uq.astype(bf), 'w_v': w_v.astype(bf), 'w_kt': w_kt.astype(bf),
        'cmp_pe': p['cmp_pe'][l].reshape(2, 1, CMP_BLOCK * hd),
        'cmp_w1': p['cmp_w1'][l].astype(bf),
        'cmp_w2': cmp_w2p.astype(bf),
        'cmp_w2t': jnp.swapaxes(cmp_w2p, 1, 2).astype(bf),
        'nsa_w_o': p['nsa_w_o'][l].astype(bf), 'mla_w_o': p['mla_w_o'][l].astype(bf),
        'w_out': p['w_out'][l].astype(bf),
        'mlp_norm': p['mlp_norm'][l][None, :],
        'w_up': p['w_up'][l].astype(bf), 'w_down': p['w_down'][l].astype(bf),
    }


def _tables(S):
    pos = jnp.arange(S, dtype=jnp.int32).astype(jnp.float32)
    hd = NSA_HEAD_DIM

    def angles(d):
        inv = ROPE_THETA ** (-jnp.arange(0, d, 2, dtype=jnp.float32) / d)
        ang = pos[:, None] * inv[None, :]
        return jnp.cos(ang), jnp.sin(ang)

    cos, sin = angles(hd)
    z = jnp.zeros_like(cos)
    scale = hd ** -0.5 * LOG2E
    pad = jnp.zeros((S, LANES - hd), jnp.float32)
    cq = jnp.concatenate([cos, cos, pad], axis=1) * scale
    s1q = jnp.concatenate([z, sin, pad], axis=1) * scale
    s2q = jnp.concatenate([-sin, z, pad], axis=1) * scale
    cc = jnp.concatenate([cos, cos, cos, cos], axis=1)
    s1c = jnp.concatenate([z, sin, z, sin], axis=1)
    s2c = jnp.concatenate([-sin, z, -sin, z], axis=1)

    cos16, sin16 = angles(MLA_ROPE)
    z16 = jnp.zeros_like(cos16)
    sm = (MLA_NOPE + MLA_ROPE) ** -0.5 * LOG2E
    one =jnp.ones((S, MLA_NOPE), jnp.float32)
    z64 = jnp.zeros((S, MLA_NOPE), jnp.float32)
    pad32 = jnp.zeros((S, LANES - MLA_NOPE - MLA_ROPE), jnp.float32)
    cm = jnp.concatenate([one, cos16, cos16, pad32], axis=1) * sm
    s1m = jnp.concatenate([z64, z16, sin16, pad32], axis=1) * sm
    s2m = jnp.concatenate([z64, -sin16, z16, pad32], axis=1) * sm
    return {'cq': cq, 's1q': s1q, 's2q': s2q, 'cc': cc, 's1c': s1c, 's2c': s2c,
            'cos_t': cos.T, 'sin_t': sin.T, 'cm': cm, 's1m': s1m, 's2m': s2m,
            'cos_t16': cos16.T, 'sin_t16': sin16.T}


def _overlap_t(S):
    nc = S // CMP_STRIDE
    n_cmp = (S - CMP_BLOCK) // CMP_STRIDE + 1
    ns = S // SLC_BLOCK
    start = np.arange(nc) * CMP_STRIDE
    sel = np.arange(ns) * SLC_BLOCK
    lo = np.maximum(start[None, :], sel[:, None])
    hi = np.minimum(start[None, :] + CMP_BLOCK, sel[:, None] + SLC_BLOCK)
    ov = np.clip(hi - lo, 0, None) / CMP_BLOCK
    ov[:, n_cmp:] = 0.0
    return jnp.asarray(ov, jnp.bfloat16)


def _cmp_blocks(cmp_in, B, S):
    hd = NSA_HEAD_DIM
    nc = S // CMP_STRIDE
    r = cmp_in.reshape(B, nc, CMP_STRIDE, 2 * NSA_KV_GROUPS, hd)
    r = r.transpose(0, 3, 1, 2, 4).reshape(B, 2 * NSA_KV_GROUPS, nc, CMP_STRIDE * hd)
    nxt = jnp.concatenate([r[:, :, 1:], jnp.zeros_like(r[:, :, :1])], axis=2)
    return jnp.concatenate([r, nxt], axis=-1)


def _forward(x, params, depth):
    B, S, D = x.shape
    T = B * S
    tabs = _tables(S)
    ovt = _overlap_t(S)
    x2 = x.reshape(T, D)
    final_norm = params['final_norm'][None, :]
    G = NSA_KV_GROUPS
    for l in range(depth):
        lw = _prep_layer(params, l)
        (qn, cmp_in, vs, vw, ng, ga, gb, kts, ktw, qm, vm, ktm) = _in_proj(x2, lw, tabs, B, S)
        kct, vc = _compress(_cmp_blocks(cmp_in, B, S), lw)
        o_cmp, selb = _cmp_select(qn, kct, vc, ng, ovt, B, S)
        o_slc = _flash(qn, kts, vs, B, S, hpk=NSA_REP, kv_per_step=1, selb=selb,
                       gate=ng, gate_col=G, name="flash_slc")
        o_win = _flash(qn, ktw, vw, B, S, hpk=NSA_REP, kv_per_step=1, window=WINDOW,
                       gate=ng, gate_col=2 * G, name="flash_win")
        o_b = _flash(qm, ktm, vm, B, S, hpk=1, kv_per_step=4, name="flash_mla")
        x2 = _merge(o_cmp, o_slc, o_win, o_b, ga, gb, x2, lw)
        x2 = _mlp(x2, lw, final_norm, final=(l == depth - 1))
    return x2.reshape(B, S, D)


def kernel(x, attn_norm, w_in, cmp_pe, cmp_w1, cmp_w2, nsa_w_o, mla_q_norm, mla_kv_norm, mla_w_uq,
           mla_w_ukv, mla_w_o, w_out, mlp_norm, w_up, w_down, final_norm):
    params = dict(attn_norm=attn_norm, w_in=w_in, cmp_pe=cmp_pe, cmp_w1=cmp_w1, cmp_w2=cmp_w2,
                  nsa_w_o=nsa_w_o, mla_q_norm=mla_q_norm, mla_kv_norm=mla_kv_norm, mla_w_uq=mla_w_uq,
                  mla_w_ukv=mla_w_ukv, mla_w_o=mla_w_o, w_out=w_out, mlp_norm=mlp_norm, w_up=w_up,
                  w_down=w_down, final_norm=final_norm)
    return _forward(x, params, w_in.shape[0])
```

```python
import functools

import numpy as np
import jax
import jax.numpy as jnp
from jax import lax
from jax.experimental import pallas as pl
from jax.experimental.pallas import tpu as pltpu

EPS = 1e-6
ROPE_THETA = 10000.0
NEG = -1e30
TINY = 1e-30
LOG2E = 1.4426950408889634

D_MODEL = 1024
NSA_HEADS = 8
NSA_KV_GROUPS = 2
NSA_REP = NSA_HEADS // NSA_KV_GROUPS
NSA_HEAD_DIM = 64
CMP_BLOCK = 32
CMP_STRIDE = 16
CMP_HIDDEN = 4 * NSA_HEAD_DIM
SLC_BLOCK = 64
SLC_TOPK = 16
N_LOCAL_FORCED = 2
FORCE_BONUS = 1e4
WINDOW = 512

MLA_HEADS = 8
MLA_Q_LORA = 256
MLA_KV_LORA = 128
MLA_NOPE = 64
MLA_ROPE = 32
MLA_V = 64
D_FF = 4 * D_MODEL

LANES = 128
SUBLANES = 8
VMEM_LIMIT_BYTES = 56 * 1024 * 1024

TOKEN_TILE = 512
FLASH_TILE = 512
CMP_TILE = 256
BIAS_ROWS = LANES - NSA_HEAD_DIM
GATE_ROWS = 32

_OFF_Q = 0
_OFF_KVC = _OFF_Q + NSA_HEADS * NSA_HEAD_DIM
_OFF_KVS = _OFF_KVC + 2 * NSA_KV_GROUPS * NSA_HEAD_DIM
_OFF_KVW = _OFF_KVS + 2 * NSA_KV_GROUPS * NSA_HEAD_DIM
_OFF_NG = _OFF_KVW + 2 * NSA_KV_GROUPS * NSA_HEAD_DIM
_OFF_CQ = _OFF_NG + 3 * NSA_HEADS
_OFF_CKV = _OFF_CQ + MLA_Q_LORA
_OFF_KR = _OFF_CKV + MLA_KV_LORA
_OFF_GA = _OFF_KR + MLA_ROPE
_OFF_GB = _OFF_GA + D_MODEL

_R_CMP = 0
_R_KS = _R_CMP + 2 * NSA_KV_GROUPS * NSA_HEAD_DIM
_R_KW = _R_KS + NSA_KV_GROUPS * LANES
_R_CQ = _R_KW + NSA_KV_GROUPS * LANES
_R_CKV = _R_CQ + MLA_Q_LORA
_R_KPE = _R_CKV + MLA_KV_LORA
_R_GA = _R_KPE + LANES
_R_GB = _R_GA + D_MODEL
_R_END = _R_GB + D_MODEL

_T_Q = 0
_T_VS = _T_Q + NSA_HEADS * NSA_HEAD_DIM
_T_VW = _T_VS + NSA_KV_GROUPS * NSA_HEAD_DIM
_T_NG = _T_VW + NSA_KV_GROUPS * NSA_HEAD_DIM
_T_END = _T_NG + GATE_ROWS


def _nt_dot(a, b):
    return lax.dot_general(a, b, (((1,), (1,)), ((), ())), preferred_element_type=jnp.float32)


def _dot(a, b):
    return jnp.dot(a, b, preferred_element_type=jnp.float32)


def _rms(xf, g):
    return xf * lax.rsqrt(jnp.mean(xf * xf, axis=-1, keepdims=True) + EPS) * g


def _rope_rows(y, c, s1, s2, sh):
    return y * c + pltpu.roll(y, sh, 1) * s1 + pltpu.roll(y, LANES - sh, 1) * s2


def _rope_cols(y, cos_t, sin_t):
    h = y.shape[0] // 2
    a, b = y[:h], y[h:]
    return jnp.concatenate([a * cos_t - b * sin_t, a * sin_t + b * cos_t], axis=0)


def _in_proj_kernel(x_ref, g_ref, w_ref, wt_ref, qng_ref, kvng_ref, wuqt_ref, wvt_ref, wk_ref,
                    cc_ref, s1c_ref, s2c_ref, ck_ref, s1k_ref, s2k_ref, ckp_ref, s1kp_ref, s2kp_ref,
                    cosq_ref, sinq_ref, cosm_ref, sinm_ref,
                    cmp_ref, ks_ref, kw_ref, ga_ref, gb_ref, km_ref,
                    qt_ref, vst_ref, vwt_ref, gt_ref, qmt_ref, vmt_ref, *, tm, tk, s_tiles):
    si = pl.program_id(0) % s_tiles
    hd = NSA_HEAD_DIM
    bf = jnp.bfloat16
    xn = _rms(x_ref[...], g_ref[...]).astype(bf)

    y = _dot(xn, w_ref[:, _R_CMP:_R_KS])
    cmp_ref[:, :LANES] = _rope_rows(y[:, :LANES], cc_ref[...], s1c_ref[...], s2c_ref[...], hd // 2)
    cmp_ref[:, LANES:] = y[:, LANES:]

    lane = lax.broadcasted_iota(jnp.int32, (tm, LANES), 1)
    pos = si * tm + lax.broadcasted_iota(jnp.int32, (tm, LANES), 0)
    sel_cols = jnp.where(lane - hd == pos // SLC_BLOCK, 1.0, 0.0)
    ck, s1k, s2k = ck_ref[...], s1k_ref[...], s2k_ref[...]
    y = _dot(xn, w_ref[:, _R_KS:_R_KW])
    for g in range(NSA_KV_GROUPS):
        blk = _rope_rows(y[:, g * LANES:(g + 1) * LANES], ck, s1k, s2k, hd // 2)
        ks_ref[:, g * LANES:(g + 1) * LANES] = (blk + sel_cols).astype(bf)
    y = _dot(xn, w_ref[:, _R_KW:_R_CQ])
    for g in range(NSA_KV_GROUPS):
        kw_ref[:, g * LANES:(g + 1) * LANES] = _rope_rows(y[:, g * LANES:(g + 1) * LANES], ck, s1k, s2k,
                                                          hd // 2).astype(bf)

    ga_ref[...] = jax.nn.sigmoid(_dot(xn, w_ref[:, _R_GA:_R_GB])).astype(bf)
    gb_ref[...] = jax.nn.sigmoid(_dot(xn, w_ref[:, _R_GB:_R_END])).astype(bf)

    cqn = _rms(_dot(xn, w_ref[:, _R_CQ:_R_CKV]), qng_ref[...]).astype(bf)
    ckvn = _rms(_dot(xn, w_ref[:, _R_CKV:_R_KPE]), kvng_ref[...]).astype(bf)
    kpe = _rope_rows(_dot(xn, w_ref[:, _R_KPE:_R_GA]), ckp_ref[...], s1kp_ref[...], s2kp_ref[...], MLA_ROPE // 2)
    y = _dot(ckvn, wk_ref[...])
    for h in range(MLA_HEADS):
        km_ref[:, h * LANES:(h + 1) * LANES] = (y[:, h * LANES:(h + 1) * LANES] + kpe).astype(bf)

    yt = _nt_dot(wt_ref[...], xn)
    cosq, sinq = cosq_ref[...], sinq_ref[...]
    zero_rows = jnp.zeros((LANES - hd, tm), jnp.float32)
    for h in range(NSA_HEADS):
        q = _rope_cols(yt[_T_Q + h * hd:_T_Q + (h + 1) * hd], cosq, sinq)
        qt_ref[0, h] = jnp.concatenate([q, zero_rows], axis=0).astype(bf)
    ones_rows = jnp.where(lax.broadcasted_iota(jnp.int32, (LANES - hd, tm), 0) == 0, 1.0, 0.0)
    for g in range(NSA_KV_GROUPS):
        vs = jnp.concatenate([yt[_T_VS + g * hd:_T_VS + (g + 1) * hd], ones_rows], axis=0).astype(bf)
        vw = jnp.concatenate([yt[_T_VW + g * hd:_T_VW + (g + 1) * hd], ones_rows], axis=0).astype(bf)
        for c in range(tm // tk):
            vst_ref[0, g, c] = vs[:, c * tk:(c + 1) * tk]
            vwt_ref[0, g, c] = vw[:, c * tk:(c + 1) * tk]
    gt_ref[...] = jax.nn.sigmoid(yt[_T_NG:_T_END])

    dq = MLA_NOPE + MLA_ROPE
    sm = dq ** -0.5 * LOG2E
    ymt = _nt_dot(wuqt_ref[...], cqn)
    cosm, sinm = cosm_ref[...], sinm_ref[...]
    pad_rows = jnp.zeros((LANES - dq, tm), jnp.float32)
    for h in range(MLA_HEADS):
        nope = ymt[h * dq:h * dq + MLA_NOPE] * sm
        pe = _rope_cols(ymt[h * dq + MLA_NOPE:(h + 1) * dq], cosm, sinm)
        qmt_ref[0, h] = jnp.concatenate([nope, pe, pad_rows], axis=0).astype(bf)
    yvt = _nt_dot(wvt_ref[...], ckvn)
    for h in range(MLA_HEADS):
        vm = jnp.concatenate([yvt[h * MLA_V:(h + 1) * MLA_V], ones_rows], axis=0).astype(bf)
        for c in range(tm // tk):
            vmt_ref[0, h, c] = vm[:, c * tk:(c + 1) * tk]


def _in_proj(x2, lw, tabs, B, S):
    T = B * S
    tm, tk = TOKEN_TILE, FLASH_TILE
    s_tiles = S // tm
    nkt = S // tk
    bf = jnp.bfloat16

    def full(a):
        return pl.BlockSpec(a.shape, lambda i, _n=a.ndim: (0,) * _n)

    def rows(width):
        return pl.BlockSpec((tm, width), lambda i: (i, 0))

    def tab_rows(a):
        return pl.BlockSpec((tm, a.shape[1]), lambda i: (i % s_tiles, 0))

    def tab_cols(a):
        return pl.BlockSpec((a.shape[0], tm), lambda i: (0, i % s_tiles))

    def qt_spec(n):
        return pl.BlockSpec((1, n, LANES, tm), lambda i: (i // s_tiles, 0, 0, i % s_tiles))

    def vt_spec(n):
        return pl.BlockSpec((1, n, tm // tk, LANES, tk), lambda i: (i // s_tiles, 0, i % s_tiles, 0, 0))

    weights = [lw['attn_norm'], lw['w_row'], lw['w_t'], lw['q_norm'], lw['kv_norm'],
               lw['w_uqt'], lw['w_vt'], lw['w_k']]
    row_tabs = [tabs[k] for k in ('cc', 's1c', 's2c', 'ck', 's1k', 's2k', 'ckp', 's1kp', 's2kp')]
    col_tabs = [tabs[k] for k in ('cosq_t', 'sinq_t', 'cosm_t', 'sinm_t')]
    in_specs = ([rows(D_MODEL)] + [full(w) for w in weights] + [tab_rows(t) for t in row_tabs]
                + [tab_cols(t) for t in col_tabs])
    G, H = NSA_KV_GROUPS, MLA_HEADS
    out_shape = [
        jax.ShapeDtypeStruct((T, 2 * G * NSA_HEAD_DIM), jnp.float32),
        jax.ShapeDtypeStruct((T, G * LANES), bf),
        jax.ShapeDtypeStruct((T, G * LANES), bf),
        jax.ShapeDtypeStruct((T, D_MODEL), bf),
        jax.ShapeDtypeStruct((T, D_MODEL), bf),
        jax.ShapeDtypeStruct((T, H * LANES), bf),
        jax.ShapeDtypeStruct((B, NSA_HEADS, LANES, S), bf),
        jax.ShapeDtypeStruct((B, G, nkt, LANES, tk), bf),
        jax.ShapeDtypeStruct((B, G, nkt, LANES, tk), bf),
        jax.ShapeDtypeStruct((GATE_ROWS, T), jnp.float32),
        jax.ShapeDtypeStruct((B, H, LANES, S), bf),
        jax.ShapeDtypeStruct((B, H, nkt, LANES, tk), bf),
    ]
    out_specs = [rows(2 * G * NSA_HEAD_DIM), rows(G * LANES), rows(G * LANES), rows(D_MODEL), rows(D_MODEL),
                 rows(H * LANES), qt_spec(NSA_HEADS), vt_spec(G), vt_spec(G),
                 pl.BlockSpec((GATE_ROWS, tm), lambda i: (0, i)), qt_spec(H), vt_spec(H)]
    return pl.pallas_call(
        functools.partial(_in_proj_kernel, tm=tm, tk=tk, s_tiles=s_tiles),
        grid=(T // tm,),
        in_specs=in_specs,
        out_specs=out_specs,
        out_shape=out_shape,
        compiler_params=pltpu.CompilerParams(dimension_semantics=("arbitrary",),
                                             vmem_limit_bytes=VMEM_LIMIT_BYTES),
        name="in_proj",
    )(x2, *weights, *row_tabs, *col_tabs)


def _compress_kernel(blk_ref, pe_ref, w1_ref, w2_ref, w2t_ref, ot_ref, or_ref):
    tb = (blk_ref[0, 0] + pe_ref[0]).astype(jnp.bfloat16)
    h = _dot(tb, w1_ref[0])
    h = (h * jax.nn.sigmoid(h)).astype(jnp.bfloat16)
    ot_ref[0, 0] = _nt_dot(w2t_ref[0], h).astype(jnp.bfloat16)
    or_ref[0, 0] = _dot(h, w2_ref[0]).astype(jnp.bfloat16)


def _compress(blocks, lw):
    B, _, NC, K = blocks.shape
    bf = jnp.bfloat16
    return pl.pallas_call(
        _compress_kernel,
        grid=(B, 2 * NSA_KV_GROUPS),
        in_specs=[
            pl.BlockSpec((1, 1, NC, K), lambda b, c: (b, c, 0, 0)),
            pl.BlockSpec((1, 1, K), lambda b, c: (c // NSA_KV_GROUPS, 0, 0)),
            pl.BlockSpec((1, K, CMP_HIDDEN), lambda b, c: (c // NSA_KV_GROUPS, 0, 0)),
            pl.BlockSpec((1, CMP_HIDDEN, LANES), lambda b, c: (c // NSA_KV_GROUPS, 0, 0)),
            pl.BlockSpec((1, LANES, CMP_HIDDEN), lambda b, c: (c // NSA_KV_GROUPS, 0, 0)),
        ],
        out_specs=[pl.BlockSpec((1, 1, LANES, NC), lambda b, c: (b, c, 0, 0)),
                   pl.BlockSpec((1, 1, NC, LANES), lambda b, c: (b, c, 0, 0))],
        out_shape=[jax.ShapeDtypeStruct((B, 2 * NSA_KV_GROUPS, LANES, NC), bf),
                   jax.ShapeDtypeStruct((B, 2 * NSA_KV_GROUPS, NC, LANES), bf)],
        compiler_params=pltpu.CompilerParams(dimension_semantics=("arbitrary", "arbitrary"),
                                             vmem_limit_bytes=VMEM_LIMIT_BYTES),
        name="compress",
    )(blocks, lw['cmp_pe'], lw['cmp_w1'], lw['cmp_w2'], lw['cmp_w2t'])


def _cmp_select_kernel(qt_ref, kc_ref, vct_ref, gate_ref, ovt_ref, o_ref, selb_ref, score_scr, *, tq, nc, ns):
    g = pl.program_id(1)
    qi = pl.program_id(2)
    hd = NSA_HEAD_DIM
    pos = qi * tq + lax.broadcasted_iota(jnp.int32, (nc, tq), 1)
    blk_c = lax.broadcasted_iota(jnp.int32, (nc, tq), 0)
    valid = (blk_c * CMP_STRIDE + (CMP_BLOCK - 1)) <= pos
    kc = kc_ref[0, 0]
    vct = vct_ref[0, 0]

    psum = jnp.zeros((nc, tq), jnp.float32)
    outs = []
    for r in range(NSA_REP):
        s = _dot(kc, qt_ref[0, r])
        sf = jnp.where(valid, s, NEG)
        m = jnp.max(sf, axis=0, keepdims=True)
        e = jnp.where(valid, jnp.exp2(sf - m), 0.0)
        p = e / jnp.maximum(jnp.sum(e, axis=0, keepdims=True), TINY)
        psum = psum + p
        o = _dot(vct, p.astype(jnp.bfloat16))
        gate = jnp.where(g == 0, gate_ref[r:r + 1, :], gate_ref[NSA_REP + r:NSA_REP + r + 1, :])
        outs.append(o[:hd] * gate)
    o_ref[...] = jnp.concatenate(outs, axis=0).T.astype(jnp.bfloat16)

    hi = psum.astype(jnp.bfloat16)
    lo = (psum - hi.astype(jnp.float32)).astype(jnp.bfloat16)
    ovt = ovt_ref[...]
    imp = _dot(ovt, hi) + _dot(ovt, lo)
    post = qi * tq + lax.broadcasted_iota(jnp.int32, (ns, tq), 1)
    blk = lax.broadcasted_iota(jnp.int32, (ns, tq), 0)
    cur = post // SLC_BLOCK
    causal = blk <= cur
    forced = (blk == 0) | ((cur - blk >= 0) & (cur - blk < N_LOCAL_FORCED))
    score = jnp.where(causal, imp + jnp.where(forced, FORCE_BONUS, 0.0), NEG)
    score_scr[...] = score

    sub_id = lax.broadcasted_iota(jnp.int32, (SUBLANES, tq), 0)
    counts = []
    for rg in range(ns // SUBLANES):
        sub = score[rg * SUBLANES:(rg + 1) * SUBLANES]
        c = jnp.zeros((SUBLANES, tq), jnp.float32)
        for i in range(ns):
            bi = score_scr[i:i + 1, :]
            if i < rg * SUBLANES:
                ahead = bi >= sub
            elif i >= (rg + 1) * SUBLANES:
                ahead = bi > sub
            else:
                ahead = (bi > sub) | ((bi == sub) & (sub_id > (i - rg * SUBLANES)))
            c = c + jnp.where(ahead, 1.0, 0.0)
        counts.append(c)
    cnt = jnp.concatenate(counts, axis=0)
    sel = (cnt < float(min(SLC_TOPK, ns))) & causal
    parts = [jnp.where(sel, 0.0, NEG)]
    if BIAS_ROWS - ns > 0:
        parts.append(jnp.zeros((BIAS_ROWS - ns, tq), jnp.float32))
    selb_ref[0, 0] = jnp.concatenate(parts, axis=0).astype(jnp.bfloat16)


def _cmp_select(qt, kc, vct, gt, ovt, B, S):
    T = B * S
    tq = CMP_TILE
    nq = S // tq
    nc = kc.shape[2]
    ns = S // SLC_BLOCK
    assert ns <= BIAS_ROWS
    G = NSA_KV_GROUPS
    bf = jnp.bfloat16
    return pl.pallas_call(
        functools.partial(_cmp_select_kernel, tq=tq, nc=nc, ns=ns),
        grid=(B, G, nq),
        in_specs=[
            pl.BlockSpec((1, NSA_REP, LANES, tq), lambda b, g, i: (b, g, 0, i)),
            pl.BlockSpec((1, 1, nc, LANES), lambda b, g, i: (b, g, 0, 0)),
            pl.BlockSpec((1, 1, LANES, nc), lambda b, g, i: (b, G + g, 0, 0)),
            pl.BlockSpec((NSA_HEADS, tq), lambda b, g, i: (0, b * nq + i)),
            pl.BlockSpec((ns, nc), lambda b, g, i: (0, 0)),
        ],
        out_specs=[pl.BlockSpec((tq, NSA_REP * NSA_HEAD_DIM), lambda b, g, i: (b * nq + i, g)),
                   pl.BlockSpec((1, 1, BIAS_ROWS, tq), lambda b, g, i: (b, g, 0, i))],
        out_shape=[jax.ShapeDtypeStruct((T, NSA_HEADS * NSA_HEAD_DIM), bf),
                   jax.ShapeDtypeStruct((B, G, BIAS_ROWS, S), bf)],
        scratch_shapes=[pltpu.VMEM((ns, tq), jnp.float32)],
        compiler_params=pltpu.CompilerParams(dimension_semantics=("arbitrary",) * 3,
                                             vmem_limit_bytes=VMEM_LIMIT_BYTES),
        name="cmp_select",
    )(qt, kc, vct, gt, ovt)


def _flash_kernel(*refs, hpk, kv_per_step, tq, tk, window, has_sel, has_gate):
    refs = list(refs)
    qt_ref = refs.pop(0)
    selb_ref = refs.pop(0) if has_sel else None
    k_ref = refs.pop(0)
    vt_ref = refs.pop(0)
    bias_ref = refs.pop(0)
    gate_ref = refs.pop(0) if has_gate else None
    o_ref, q_scr, p_scr, m_scr, alpha_scr, acc_scr = refs
    nh = hpk * kv_per_step
    hd = NSA_HEAD_DIM
    j = pl.program_id(1)
    qi = pl.program_id(2)

    for h in range(nh):
        q = qt_ref[0, h]
        if has_sel:
            q = jnp.concatenate([q[:hd], selb_ref[0, 0]], axis=0)
        q_scr[h] = q
    m_scr[...] = jnp.full(m_scr.shape, NEG, jnp.float32)
    acc_scr[...] = jnp.zeros(acc_scr.shape, jnp.float32)

    def values(kt, h, slot):
        vtile = vt_ref[0, h // hpk, kt]
        acc_scr[h] = alpha_scr[slot, h] * acc_scr[h] + _dot(vtile, p_scr[slot, h])

    def tile_step(kt, bias_idx, prev, slot):
        s_all = []
        for p in range(kv_per_step):
            ktile = k_ref[pl.ds(pl.multiple_of(kt * tk, tk), tk), p * LANES:(p + 1) * LANES]
            for r in range(hpk):
                s = _dot(ktile, q_scr[p * hpk + r])
                if bias_idx is not None:
                    s = s + bias_ref[bias_idx]
                s_all.append(s)
        for h in range(nh):
            if prev is not None:
                values(prev, h, 1 - slot)
            s = s_all[h]
            m_prev = m_scr[h]
            m_new = jnp.maximum(m_prev, jnp.max(s, axis=0, keepdims=True))
            alpha_scr[slot, h] = jnp.exp2(m_prev - m_new)
            p_scr[slot, h] = jnp.exp2(s - m_new).astype(jnp.bfloat16)
            m_scr[h] = m_new

    tile_step(qi, 0, None, 0)
    if window is None:
        lo, n_extra, bias_idx = 0, qi, None
    else:
        lo, n_extra, bias_idx = qi - 1, jnp.minimum(qi, 1), 1

    def pair(i, carry):
        kt = lo + 2 * i
        tile_step(kt, bias_idx, jnp.where(i == 0, qi, kt - 1), 1)
        tile_step(kt + 1, bias_idx, kt, 0)
        return carry

    lax.fori_loop(0, n_extra // 2, pair, 0)
    last = lo + n_extra - 1

    @pl.when(n_extra % 2 == 1)
    def _():
        tile_step(last, bias_idx, jnp.where(n_extra == 1, qi, last - 1), 1)
        for h in range(nh):
            values(last, h, 1)

    @pl.when(n_extra % 2 == 0)
    def _():
        for h in range(nh):
            values(jnp.where(n_extra == 0, qi, last), h, 0)

    outs = []
    for h in range(nh):
        acc = acc_scr[h]
        o = acc[:hd] / jnp.maximum(acc[hd:hd + 1], TINY)
        if has_gate:
            o = o * jnp.where(j == 0, gate_ref[h:h + 1, :], gate_ref[nh + h:nh + h + 1, :])
        outs.append(o)
    o_ref[...] = jnp.concatenate(outs, axis=0).T.astype(jnp.bfloat16)


def _tile_biases(t):
    k = np.arange(t)[:, None]
    q = np.arange(t)[None, :]
    return jnp.asarray(np.stack([np.where(k <= q, 0.0, NEG), np.where(k > q, 0.0, NEG)]), jnp.float32)


def _flash(qt, k, vt, B, S, *, hpk, kv_per_step, window=None, selb=None, gate=None, gate_blk=0, name):
    T = B * S
    tq = tk = FLASH_TILE
    assert window is None or window == tk
    nq = S // tq
    nkv = vt.shape[1]
    nh = hpk * kv_per_step
    steps = nkv // kv_per_step
    hd = NSA_HEAD_DIM
    in_specs = [pl.BlockSpec((1, nh, LANES, tq), lambda b, j, i: (b, j, 0, i))]
    args = [qt]
    if selb is not None:
        in_specs.append(pl.BlockSpec((1, 1, BIAS_ROWS, tq), lambda b, j, i: (b, j, 0, i)))
        args.append(selb)
    in_specs.append(pl.BlockSpec((S, kv_per_step * LANES), lambda b, j, i: (b, j)))
    in_specs.append(pl.BlockSpec((1, kv_per_step, S // tk, LANES, tk), lambda b, j, i: (b, j, 0, 0, 0)))
    in_specs.append(pl.BlockSpec((2, tk, tq), lambda b, j, i: (0, 0, 0)))
    args += [k, vt, _tile_biases(tq)]
    if gate is not None:
        in_specs.append(pl.BlockSpec((NSA_HEADS, tq), lambda b, j, i: (gate_blk, b * nq + i)))
        args.append(gate)
    return pl.pallas_call(
        functools.partial(_flash_kernel, hpk=hpk, kv_per_step=kv_per_step, tq=tq, tk=tk, window=window,
                          has_sel=selb is not None, has_gate=gate is not None),
        grid=(B, steps, nq),
        in_specs=in_specs,
        out_specs=pl.BlockSpec((tq, nh * hd), lambda b, j, i: (b * nq + i, j)),
        out_shape=jax.ShapeDtypeStruct((T, nkv * hpk * hd), jnp.bfloat16),
        scratch_shapes=[pltpu.VMEM((nh, LANES, tq), jnp.bfloat16),
                        pltpu.VMEM((2, nh, tk, tq), jnp.bfloat16),
                        pltpu.VMEM((nh, 1, tq), jnp.float32),
                        pltpu.VMEM((2, nh, 1, tq), jnp.float32),
                        pltpu.VMEM((nh, LANES, tq), jnp.float32)],
        compiler_params=pltpu.CompilerParams(dimension_semantics=("arbitrary",) * 3,
                                             vmem_limit_bytes=VMEM_LIMIT_BYTES),
        name=name,
    )(*args)


def _merge_kernel(oc_ref, os_ref, ow_ref, ob_ref, ga_ref, gb_ref, x_ref, wa_ref, wb_ref, wo_ref, out_ref):
    f32 = jnp.float32
    o_a = (oc_ref[...].astype(f32) + os_ref[...].astype(f32) + ow_ref[...].astype(f32)).astype(jnp.bfloat16)
    merged = (ga_ref[...].astype(f32) * _dot(o_a, wa_ref[...])
              + gb_ref[...].astype(f32) * _dot(ob_ref[...], wb_ref[...]))
    out_ref[...] = x_ref[...] + _dot(merged.astype(jnp.bfloat16), wo_ref[...])


def _merge(o_cmp, o_slc, o_win, o_b, ga, gb, x2, lw):
    T = x2.shape[0]
    tm = TOKEN_TILE

    def rows(width):
        return pl.BlockSpec((tm, width), lambda i: (i, 0))

    def full(a):
        return pl.BlockSpec(a.shape, lambda i: (0, 0))

    w = NSA_HEADS * NSA_HEAD_DIM
    return pl.pallas_call(
        _merge_kernel,
        grid=(T // tm,),
        in_specs=[rows(w), rows(w), rows(w), rows(MLA_HEADS * MLA_V), rows(D_MODEL), rows(D_MODEL),
                  rows(D_MODEL), full(lw['nsa_w_o']), full(lw['mla_w_o']), full(lw['w_out'])],
        out_specs=rows(D_MODEL),
        out_shape=jax.ShapeDtypeStruct((T, D_MODEL), jnp.float32),
        compiler_params=pltpu.CompilerParams(dimension_semantics=("arbitrary",),
                                             vmem_limit_bytes=VMEM_LIMIT_BYTES),
        name="merge",
    )(o_cmp, o_slc, o_win, o_b, ga, gb, x2, lw['nsa_w_o'], lw['mla_w_o'], lw['w_out'])


def _mlp_kernel(x_ref, g_ref, wu_ref, wd_ref, gf_ref, out_ref, *, ff_chunk, final):
    x = x_ref[...]
    hn = _rms(x, g_ref[...]).astype(jnp.bfloat16)
    acc = jnp.zeros(x.shape, jnp.float32)
    for c in range(D_FF // ff_chunk):
        h = jnp.maximum(_dot(hn, wu_ref[:, c * ff_chunk:(c + 1) * ff_chunk]), 0.0)
        acc = acc + _dot((h * h).astype(jnp.bfloat16), wd_ref[c * ff_chunk:(c + 1) * ff_chunk, :])
    y = x + acc
    if final:
        y = _rms(y, gf_ref[...])
    out_ref[...] = y


def _mlp(x2, lw, final_norm, final):
    T = x2.shape[0]
    tm = TOKEN_TILE

    def full(a):
        return pl.BlockSpec(a.shape, lambda i: (0, 0))

    return pl.pallas_call(
        functools.partial(_mlp_kernel, ff_chunk=1024, final=final),
        grid=(T // tm,),
        in_specs=[pl.BlockSpec((tm, D_MODEL), lambda i: (i, 0)), full(lw['mlp_norm']),
                  full(lw['w_up']), full(lw['w_down']), full(final_norm)],
        out_specs=pl.BlockSpec((tm, D_MODEL), lambda i: (i, 0)),
        out_shape=jax.ShapeDtypeStruct((T, D_MODEL), jnp.float32),
        compiler_params=pltpu.CompilerParams(dimension_semantics=("arbitrary",),
                                             vmem_limit_bytes=VMEM_LIMIT_BYTES),
        name="mlp",
    )(x2, lw['mlp_norm'], lw['w_up'], lw['w_down'], final_norm)


def _pad_cols(w, width, left=0):
    return jnp.pad(w, ((0, 0), (left, width - left - w.shape[1])))


def _prep_layer(p, l):
    bf = jnp.bfloat16
    hd = NSA_HEAD_DIM
    G = NSA_KV_GROUPS
    w = p['w_in'][l]
    cols = [w[:, _OFF_KVC:_OFF_KVS]]
    for off in (_OFF_KVS, _OFF_KVW):
        for g in range(G):
            cols.append(_pad_cols(w[:, off + g * hd:off + (g + 1) * hd], LANES))
    cols.append(w[:, _OFF_CQ:_OFF_CKV])
    cols.append(w[:, _OFF_CKV:_OFF_KR])
    cols.append(_pad_cols(w[:, _OFF_KR:_OFF_GA], LANES, left=MLA_NOPE))
    cols.append(w[:, _OFF_GA:_OFF_GB])
    cols.append(w[:, _OFF_GB:_OFF_GB + D_MODEL])
    w_row = jnp.concatenate(cols, axis=1).astype(bf)

    t_cols = [w[:, _OFF_Q:_OFF_KVC]]
    t_cols += [w[:, off + (G + g) * hd:off + (G + g + 1) * hd] for off in (_OFF_KVS, _OFF_KVW) for g in range(G)]
    t_cols.append(_pad_cols(w[:, _OFF_NG:_OFF_CQ], GATE_ROWS))
    w_t = jnp.concatenate(t_cols, axis=1).T.astype(bf)

    wukv = p['mla_w_ukv'][l]
    dkv = MLA_NOPE + MLA_V
    w_k = jnp.concatenate([_pad_cols(wukv[:, h * dkv:h * dkv + MLA_NOPE], LANES) for h in range(MLA_HEADS)], axis=1)
    w_vt = jnp.concatenate([wukv[:, h * dkv + MLA_NOPE:(h + 1) * dkv] for h in range(MLA_HEADS)], axis=1).T

    cmp_w2 = p['cmp_w2'][l]
    cmp_w2p = jnp.pad(cmp_w2, ((0, 0), (0, 0), (0, LANES - hd)))
    return {
        'attn_norm': p['attn_norm'][l][None, :],
        'w_row': w_row, 'w_t': w_t,
        'q_norm': p['mla_q_norm'][l][None, :], 'kv_norm': p['mla_kv_norm'][l][None, :],
        'w_uqt': p['mla_w_uq'][l].T.astype(bf), 'w_vt': w_vt.astype(bf), 'w_k': w_k.astype(bf),
        'cmp_pe': p['cmp_pe'][l].reshape(2, 1, CMP_BLOCK * hd),
        'cmp_w1': p['cmp_w1'][l].astype(bf),
        'cmp_w2': cmp_w2p.astype(bf),
        'cmp_w2t': jnp.swapaxes(cmp_w2p, 1, 2).astype(bf),
        'nsa_w_o': p['nsa_w_o'][l].astype(bf), 'mla_w_o': p['mla_w_o'][l].astype(bf),
        'w_out': p['w_out'][l].astype(bf),
        'mlp_norm': p['mlp_norm'][l][None, :],
        'w_up': p['w_up'][l].astype(bf), 'w_down': p['w_down'][l].astype(bf),
    }


def _tables(S):
    pos = jnp.arange(S, dtype=jnp.int32).astype(jnp.float32)
    hd = NSA_HEAD_DIM

    def angles(d):
        inv = ROPE_THETA ** (-jnp.arange(0, d, 2, dtype=jnp.float32) / d)
        ang = pos[:, None] * inv[None, :]
        return jnp.cos(ang), jnp.sin(ang)

    cos, sin = angles(hd)
    z = jnp.zeros_like(cos)
    pad = jnp.zeros((S, LANES - hd), jnp.float32)
    cc = jnp.concatenate([cos, cos, cos, cos], axis=1)
    s1c = jnp.concatenate([z, sin, z, sin], axis=1)
    s2c = jnp.concatenate([-sin, z, -sin, z], axis=1)
    ck = jnp.concatenate([cos, cos, pad], axis=1)
    s1k = jnp.concatenate([z, sin, pad], axis=1)
    s2k = jnp.concatenate([-sin, z, pad], axis=1)
    q_scale = hd ** -0.5 * LOG2E

    cos16, sin16 = angles(MLA_ROPE)
    z16 = jnp.zeros_like(cos16)
    z64 = jnp.zeros((S, MLA_NOPE), jnp.float32)
    pad32 = jnp.zeros((S, LANES - MLA_NOPE - MLA_ROPE), jnp.float32)
    ckp = jnp.concatenate([z64, cos16, cos16, pad32], axis=1)
    s1kp = jnp.concatenate([z64, z16, sin16, pad32], axis=1)
    s2kp = jnp.concatenate([z64, -sin16, z16, pad32], axis=1)
    m_scale = (MLA_NOPE + MLA_ROPE) ** -0.5 * LOG2E
    return {'cc': cc, 's1c': s1c, 's2c': s2c, 'ck': ck, 's1k': s1k, 's2k': s2k,
            'ckp': ckp, 's1kp': s1kp, 's2kp': s2kp,
            'cosq_t': cos.T * q_scale, 'sinq_t': sin.T * q_scale,
            'cosm_t': cos16.T * m_scale, 'sinm_t': sin16.T * m_scale}


def _overlap_t(S):
    nc = S // CMP_STRIDE
    n_cmp = (S - CMP_BLOCK) // CMP_STRIDE + 1
    ns = S // SLC_BLOCK
    start = np.arange(nc) * CMP_STRIDE
    sel = np.arange(ns) * SLC_BLOCK
    lo = np.maximum(start[None, :], sel[:, None])
    hi = np.minimum(start[None, :] + CMP_BLOCK, sel[:, None] + SLC_BLOCK)
    ov = np.clip(hi - lo, 0, None) / CMP_BLOCK
    ov[:, n_cmp:] = 0.0
    return jnp.asarray(ov, jnp.bfloat16)


def _cmp_blocks(cmp_in, B, S):
    hd = NSA_HEAD_DIM
    nc = S // CMP_STRIDE
    r = cmp_in.reshape(B, nc, CMP_STRIDE, 2 * NSA_KV_GROUPS, hd)
    r = r.transpose(0, 3, 1, 2, 4).reshape(B, 2 * NSA_KV_GROUPS, nc, CMP_STRIDE * hd)
    nxt = jnp.concatenate([r[:, :, 1:], jnp.zeros_like(r[:, :, :1])], axis=2)
    return jnp.concatenate([r, nxt], axis=-1)


def _forward(x, params, depth):
    B, S, D = x.shape
    T = B * S
    tabs = _tables(S)
    ovt = _overlap_t(S)
    x2 = x.reshape(T, D)
    final_norm = params['final_norm'][None, :]
    for l in range(depth):
        lw = _prep_layer(params, l)
        (cmp_in, ks, kw, ga, gb, km, qt, vst, vwt, gt, qmt, vmt) = _in_proj(x2, lw, tabs, B, S)
        cmp_t, cmp_r = _compress(_cmp_blocks(cmp_in, B, S), lw)
        o_cmp, selb = _cmp_select(qt, cmp_r, cmp_t, gt, ovt, B, S)
        o_slc = _flash(qt, ks, vst, B, S, hpk=NSA_REP, kv_per_step=1, selb=selb,
                       gate=gt, gate_blk=1, name="flash_slc")
        o_win = _flash(qt, kw, vwt, B, S, hpk=NSA_REP, kv_per_step=1, window=WINDOW,
                       gate=gt, gate_blk=2, name="flash_win")
        o_b = _flash(qmt, km, vmt, B, S, hpk=1, kv_per_step=4, name="flash_mla")
        x2 = _merge(o_cmp, o_slc, o_win, o_b, ga, gb, x2, lw)
        x2 = _mlp(x2, lw, final_norm, final=(l == depth - 1))
    return x2.reshape(B, S, D)


def kernel(x, attn_norm, w_in, cmp_pe, cmp_w1, cmp_w2, nsa_w_o, mla_q_norm, mla_kv_norm, mla_w_uq,
           mla_w_ukv, mla_w_o, w_out, mlp_norm, w_up, w_down, final_norm):
    params = dict(attn_norm=attn_norm, w_in=w_in, cmp_pe=cmp_pe, cmp_w1=cmp_w1, cmp_w2=cmp_w2,
                  nsa_w_o=nsa_w_o, mla_q_norm=mla_q_norm, mla_kv_norm=mla_kv_norm, mla_w_uq=mla_w_uq,
                  mla_w_ukv=mla_w_ukv, mla_w_o=mla_w_o, w_out=w_out, mlp_norm=mlp_norm, w_up=w_up,
                  w_down=w_down, final_norm=final_norm)
    return _forward(x, params, w_in.shape[0])
```

```python
import functools

import numpy as np
import jax
import jax.numpy as jnp
from jax import lax
from jax.experimental import pallas as pl
from jax.experimental.pallas import tpu as pltpu

EPS = 1e-6
ROPE_THETA = 10000.0
NEG = -1e30
TINY = 1e-30
LOG2E = 1.4426950408889634

D_MODEL = 1024
NSA_HEADS = 8
NSA_KV_GROUPS = 2
NSA_REP = NSA_HEADS // NSA_KV_GROUPS
NSA_HEAD_DIM = 64
CMP_BLOCK = 32
CMP_STRIDE = 16
CMP_HIDDEN = 4 * NSA_HEAD_DIM
SLC_BLOCK = 64
SLC_TOPK = 16
N_LOCAL_FORCED = 2
FORCE_BONUS = 1e4
WINDOW = 512

MLA_HEADS = 8
MLA_Q_LORA = 256
MLA_KV_LORA = 128
MLA_NOPE = 64
MLA_ROPE = 32
MLA_V = 64
D_FF = 4 * D_MODEL

LANES = 128
SUBLANES = 8
VMEM_LIMIT_BYTES = 56 * 1024 * 1024

TOKEN_TILE = 512
FLASH_TILE = 512
CMP_TILE = 512
BIAS_ROWS = LANES - NSA_HEAD_DIM
GATE_ROWS = 32

_OFF_Q = 0
_OFF_KVC = _OFF_Q + NSA_HEADS * NSA_HEAD_DIM
_OFF_KVS = _OFF_KVC + 2 * NSA_KV_GROUPS * NSA_HEAD_DIM
_OFF_KVW = _OFF_KVS + 2 * NSA_KV_GROUPS * NSA_HEAD_DIM
_OFF_NG = _OFF_KVW + 2 * NSA_KV_GROUPS * NSA_HEAD_DIM
_OFF_CQ = _OFF_NG + 3 * NSA_HEADS
_OFF_CKV = _OFF_CQ + MLA_Q_LORA
_OFF_KR = _OFF_CKV + MLA_KV_LORA
_OFF_GA = _OFF_KR + MLA_ROPE
_OFF_GB = _OFF_GA + D_MODEL

_R_CMP = 0
_R_KS = _R_CMP + 2 * NSA_KV_GROUPS * NSA_HEAD_DIM
_R_KW = _R_KS + NSA_KV_GROUPS * LANES
_R_CQ = _R_KW + NSA_KV_GROUPS * LANES
_R_CKV = _R_CQ + MLA_Q_LORA
_R_KPE = _R_CKV + MLA_KV_LORA
_R_GA = _R_KPE + LANES
_R_GB = _R_GA + D_MODEL
_R_END = _R_GB + D_MODEL

_T_Q = 0
_T_VS = _T_Q + NSA_HEADS * NSA_HEAD_DIM
_T_VW = _T_VS + NSA_KV_GROUPS * NSA_HEAD_DIM
_T_NG = _T_VW + NSA_KV_GROUPS * NSA_HEAD_DIM
_T_END = _T_NG + GATE_ROWS


def _nt_dot(a, b):
    return lax.dot_general(a, b, (((1,), (1,)), ((), ())), preferred_element_type=jnp.float32)


def _dot(a, b):
    return jnp.dot(a, b, preferred_element_type=jnp.float32)


def _rms(xf, g):
    return xf * lax.rsqrt(jnp.mean(xf * xf, axis=-1, keepdims=True) + EPS) * g


def _rope_rows(y, c, s1, s2, sh):
    return y * c + pltpu.roll(y, sh, 1) * s1 + pltpu.roll(y, LANES - sh, 1) * s2


def _rope_cols(y, cos_t, sin_t):
    h = y.shape[0] // 2
    a, b = y[:h], y[h:]
    return jnp.concatenate([a * cos_t - b * sin_t, a * sin_t + b * cos_t], axis=0)


def _in_proj_kernel(x_ref, g_ref, w_ref, wt_ref, qng_ref, kvng_ref, wuqt_ref, wvt_ref, wk_ref,
                    cc_ref, s1c_ref, s2c_ref, ck_ref, s1k_ref, s2k_ref, ckp_ref, s1kp_ref, s2kp_ref,
                    cosq_ref, sinq_ref, cosm_ref, sinm_ref,
                    cmp_ref, ks_ref, kw_ref, ga_ref, gb_ref, km_ref,
                    qt_ref, vst_ref, vwt_ref, gt_ref, qmt_ref, vmt_ref, *, tm, tk, s_tiles):
    si = pl.program_id(0) % s_tiles
    hd = NSA_HEAD_DIM
    bf = jnp.bfloat16
    xn = _rms(x_ref[...], g_ref[...]).astype(bf)

    y = _dot(xn, w_ref[:, _R_CMP:_R_KS])
    cmp_ref[:, :LANES] = _rope_rows(y[:, :LANES], cc_ref[...], s1c_ref[...], s2c_ref[...], hd // 2)
    cmp_ref[:, LANES:] = y[:, LANES:]

    lane = lax.broadcasted_iota(jnp.int32, (tm, LANES), 1)
    pos = si * tm + lax.broadcasted_iota(jnp.int32, (tm, LANES), 0)
    sel_cols = jnp.where(lane - hd == pos // SLC_BLOCK, 1.0, 0.0)
    ck, s1k, s2k = ck_ref[...], s1k_ref[...], s2k_ref[...]
    y = _dot(xn, w_ref[:, _R_KS:_R_KW])
    for g in range(NSA_KV_GROUPS):
        blk = _rope_rows(y[:, g * LANES:(g + 1) * LANES], ck, s1k, s2k, hd // 2)
        ks_ref[:, g * LANES:(g + 1) * LANES] = (blk + sel_cols).astype(bf)
    y = _dot(xn, w_ref[:, _R_KW:_R_CQ])
    for g in range(NSA_KV_GROUPS):
        kw_ref[:, g * LANES:(g + 1) * LANES] = _rope_rows(y[:, g * LANES:(g + 1) * LANES], ck, s1k, s2k,
                                                          hd // 2).astype(bf)

    ga_ref[...] = jax.nn.sigmoid(_dot(xn, w_ref[:, _R_GA:_R_GB])).astype(bf)
    gb_ref[...] = jax.nn.sigmoid(_dot(xn, w_ref[:, _R_GB:_R_END])).astype(bf)

    cqn = _rms(_dot(xn, w_ref[:, _R_CQ:_R_CKV]), qng_ref[...]).astype(bf)
    ckvn = _rms(_dot(xn, w_ref[:, _R_CKV:_R_KPE]), kvng_ref[...]).astype(bf)
    kpe = _rope_rows(_dot(xn, w_ref[:, _R_KPE:_R_GA]), ckp_ref[...], s1kp_ref[...], s2kp_ref[...], MLA_ROPE // 2)
    y = _dot(ckvn, wk_ref[...])
    for h in range(MLA_HEADS):
        km_ref[:, h * LANES:(h + 1) * LANES] = (y[:, h * LANES:(h + 1) * LANES] + kpe).astype(bf)

    yt = _nt_dot(wt_ref[...], xn)
    cosq, sinq = cosq_ref[...], sinq_ref[...]
    zero_rows = jnp.zeros((LANES - hd, tm), jnp.float32)
    for h in range(NSA_HEADS):
        q = _rope_cols(yt[_T_Q + h * hd:_T_Q + (h + 1) * hd], cosq, sinq)
        qt_ref[0, h] = jnp.concatenate([q, zero_rows], axis=0).astype(bf)
    ones_rows = jnp.where(lax.broadcasted_iota(jnp.int32, (LANES - hd, tm), 0) == 0, 1.0, 0.0)
    for g in range(NSA_KV_GROUPS):
        vs = jnp.concatenate([yt[_T_VS + g * hd:_T_VS + (g + 1) * hd], ones_rows], axis=0).astype(bf)
        vw = jnp.concatenate([yt[_T_VW + g * hd:_T_VW + (g + 1) * hd], ones_rows], axis=0).astype(bf)
        for c in range(tm // tk):
            vst_ref[0, g, c] = vs[:, c * tk:(c + 1) * tk]
            vwt_ref[0, g, c] = vw[:, c * tk:(c + 1) * tk]
    gt_ref[...] = jax.nn.sigmoid(yt[_T_NG:_T_END])

    dq = MLA_NOPE + MLA_ROPE
    sm = dq ** -0.5 * LOG2E
    ymt = _nt_dot(wuqt_ref[...], cqn)
    cosm, sinm = cosm_ref[...], sinm_ref[...]
    pad_rows = jnp.zeros((LANES - dq, tm), jnp.float32)
    for h in range(MLA_HEADS):
        nope = ymt[h * dq:h * dq + MLA_NOPE] * sm
        pe = _rope_cols(ymt[h * dq + MLA_NOPE:(h + 1) * dq], cosm, sinm)
        qmt_ref[0, h] = jnp.concatenate([nope, pe, pad_rows], axis=0).astype(bf)
    yvt = _nt_dot(wvt_ref[...], ckvn)
    for h in range(MLA_HEADS):
        vm = jnp.concatenate([yvt[h * MLA_V:(h + 1) * MLA_V], ones_rows], axis=0).astype(bf)
        for c in range(tm // tk):
            vmt_ref[0, h, c] = vm[:, c * tk:(c + 1) * tk]


def _in_proj(x2, lw, tabs, B, S):
    T = B * S
    tm, tk = TOKEN_TILE, FLASH_TILE
    s_tiles = S // tm
    nkt = S // tk
    bf = jnp.bfloat16

    def full(a):
        return pl.BlockSpec(a.shape, lambda i, _n=a.ndim: (0,) * _n)

    def rows(width):
        return pl.BlockSpec((tm, width), lambda i: (i, 0))

    def tab_rows(a):
        return pl.BlockSpec((tm, a.shape[1]), lambda i: (i % s_tiles, 0))

    def tab_cols(a):
        return pl.BlockSpec((a.shape[0], tm), lambda i: (0, i % s_tiles))

    def qt_spec(n):
        return pl.BlockSpec((1, n, LANES, tm), lambda i: (i // s_tiles, 0, 0, i % s_tiles))

    def vt_spec(n):
        return pl.BlockSpec((1, n, tm // tk, LANES, tk), lambda i: (i // s_tiles, 0, i % s_tiles, 0, 0))

    weights = [lw['attn_norm'], lw['w_row'], lw['w_t'], lw['q_norm'], lw['kv_norm'],
               lw['w_uqt'], lw['w_vt'], lw['w_k']]
    row_tabs = [tabs[k] for k in ('cc', 's1c', 's2c', 'ck', 's1k', 's2k', 'ckp', 's1kp', 's2kp')]
    col_tabs = [tabs[k] for k in ('cosq_t', 'sinq_t', 'cosm_t', 'sinm_t')]
    in_specs = ([rows(D_MODEL)] + [full(w) for w in weights] + [tab_rows(t) for t in row_tabs]
                + [tab_cols(t) for t in col_tabs])
    G, H = NSA_KV_GROUPS, MLA_HEADS
    out_shape = [
        jax.ShapeDtypeStruct((T, 2 * G * NSA_HEAD_DIM), jnp.float32),
        jax.ShapeDtypeStruct((T, G * LANES), bf),
        jax.ShapeDtypeStruct((T, G * LANES), bf),
        jax.ShapeDtypeStruct((T, D_MODEL), bf),
        jax.ShapeDtypeStruct((T, D_MODEL), bf),
        jax.ShapeDtypeStruct((T, H * LANES), bf),
        jax.ShapeDtypeStruct((B, NSA_HEADS, LANES, S), bf),
        jax.ShapeDtypeStruct((B, G, nkt, LANES, tk), bf),
        jax.ShapeDtypeStruct((B, G, nkt, LANES, tk), bf),
        jax.ShapeDtypeStruct((GATE_ROWS, T), jnp.float32),
        jax.ShapeDtypeStruct((B, H, LANES, S), bf),
        jax.ShapeDtypeStruct((B, H, nkt, LANES, tk), bf),
    ]
    out_specs = [rows(2 * G * NSA_HEAD_DIM), rows(G * LANES), rows(G * LANES), rows(D_MODEL), rows(D_MODEL),
                 rows(H * LANES), qt_spec(NSA_HEADS), vt_spec(G), vt_spec(G),
                 pl.BlockSpec((GATE_ROWS, tm), lambda i: (0, i)), qt_spec(H), vt_spec(H)]
    return pl.pallas_call(
        functools.partial(_in_proj_kernel, tm=tm, tk=tk, s_tiles=s_tiles),
        grid=(T // tm,),
        in_specs=in_specs,
        out_specs=out_specs,
        out_shape=out_shape,
        compiler_params=pltpu.CompilerParams(dimension_semantics=("arbitrary",),
                                             vmem_limit_bytes=VMEM_LIMIT_BYTES),
        name="in_proj",
    )(x2, *weights, *row_tabs, *col_tabs)


def _compress_kernel(blk_ref, pe_ref, w1_ref, w2_ref, w2t_ref, ot_ref, or_ref):
    tb = (blk_ref[0, 0] + pe_ref[0]).astype(jnp.bfloat16)
    h = _dot(tb, w1_ref[0])
    h = (h * jax.nn.sigmoid(h)).astype(jnp.bfloat16)
    ot_ref[0, 0] = _nt_dot(w2t_ref[0], h).astype(jnp.bfloat16)
    or_ref[0, 0] = _dot(h, w2_ref[0]).astype(jnp.bfloat16)


def _compress(blocks, lw):
    B, _, NC, K = blocks.shape
    bf = jnp.bfloat16
    return pl.pallas_call(
        _compress_kernel,
        grid=(B, 2 * NSA_KV_GROUPS),
        in_specs=[
            pl.BlockSpec((1, 1, NC, K), lambda b, c: (b, c, 0, 0)),
            pl.BlockSpec((1, 1, K), lambda b, c: (c // NSA_KV_GROUPS, 0, 0)),
            pl.BlockSpec((1, K, CMP_HIDDEN), lambda b, c: (c // NSA_KV_GROUPS, 0, 0)),
            pl.BlockSpec((1, CMP_HIDDEN, LANES), lambda b, c: (c // NSA_KV_GROUPS, 0, 0)),
            pl.BlockSpec((1, LANES, CMP_HIDDEN), lambda b, c: (c // NSA_KV_GROUPS, 0, 0)),
        ],
        out_specs=[pl.BlockSpec((1, 1, LANES, NC), lambda b, c: (b, c, 0, 0)),
                   pl.BlockSpec((1, 1, NC, LANES), lambda b, c: (b, c, 0, 0))],
        out_shape=[jax.ShapeDtypeStruct((B, 2 * NSA_KV_GROUPS, LANES, NC), bf),
                   jax.ShapeDtypeStruct((B, 2 * NSA_KV_GROUPS, NC, LANES), bf)],
        compiler_params=pltpu.CompilerParams(dimension_semantics=("arbitrary", "arbitrary"),
                                             vmem_limit_bytes=VMEM_LIMIT_BYTES),
        name="compress",
    )(blocks, lw['cmp_pe'], lw['cmp_w1'], lw['cmp_w2'], lw['cmp_w2t'])


def _cmp_select_kernel(qt_ref, kc_ref, vct_ref, gate_ref, ovt_ref, cbias_ref, o_ref, selb_ref,
                       score_scr, cnt_scr, *, tq, nc, ns):
    g = pl.program_id(1)
    qi = pl.program_id(2)
    hd = NSA_HEAD_DIM
    kc = kc_ref[0, 0]
    vct = vct_ref[0, 0]
    cbias = cbias_ref[...]
    qpos = qi * tq + lax.broadcasted_iota(jnp.int32, (1, tq), 1)
    any_valid = jnp.where(qpos >= CMP_BLOCK - 1, 1.0, 0.0)

    psum = jnp.zeros((nc, tq), jnp.float32)
    outs = []
    for r in range(NSA_REP):
        s = _dot(kc, qt_ref[0, r]) + cbias
        e = jnp.exp2(s - jnp.max(s, axis=0, keepdims=True))
        p = e * (any_valid / jnp.maximum(jnp.sum(e, axis=0, keepdims=True), TINY))
        psum = psum + p
        o = _dot(vct, p.astype(jnp.bfloat16))
        gate = jnp.where(g == 0, gate_ref[r:r + 1, :], gate_ref[NSA_REP + r:NSA_REP + r + 1, :])
        outs.append(o[:hd] * gate)
    o_ref[...] = jnp.concatenate(outs, axis=0).T.astype(jnp.bfloat16)

    hi = psum.astype(jnp.bfloat16)
    lo = (psum - hi.astype(jnp.float32)).astype(jnp.bfloat16)
    ovt = ovt_ref[...]
    imp = _dot(ovt, hi) + _dot(ovt, lo)
    post = qi * tq + lax.broadcasted_iota(jnp.int32, (ns, tq), 1)
    blk = lax.broadcasted_iota(jnp.int32, (ns, tq), 0)
    cur = post // SLC_BLOCK
    causal = blk <= cur
    forced = (blk == 0) | ((cur - blk >= 0) & (cur - blk < N_LOCAL_FORCED))
    score_scr[...] = jnp.where(causal, imp + jnp.where(forced, FORCE_BONUS, 0.0), NEG)

    n_act = (tq // SLC_BLOCK * (qi + 1) + SUBLANES - 1) // SUBLANES
    sub_id = lax.broadcasted_iota(jnp.int32, (SUBLANES, tq), 0)
    for rg in range(ns // SUBLANES):
        rows = pl.ds(rg * SUBLANES, SUBLANES)

        @pl.when(rg < n_act)
        def _():
            sub = score_scr[rows, :]
            c = jnp.zeros((SUBLANES, tq), jnp.float32)
            for i in range((rg + 1) * SUBLANES):
                bi = score_scr[i:i + 1, :]
                if i < rg * SUBLANES:
                    ahead = bi >= sub
                else:
                    ahead = (bi > sub) | ((bi == sub) & (sub_id > (i - rg * SUBLANES)))
                c = c + jnp.where(ahead, 1.0, 0.0)

            def later_group(ig, c):
                for ii in range(SUBLANES):
                    bi = score_scr[pl.ds(ig * SUBLANES + ii, 1), :]
                    c = c + jnp.where(bi > sub, 1.0, 0.0)
                return c

            cnt_scr[rows, :] = lax.fori_loop(rg + 1, n_act, later_group, c)

        @pl.when(rg >= n_act)
        def _():
            cnt_scr[rows, :] = jnp.full((SUBLANES, tq), float(ns), jnp.float32)

    sel = (cnt_scr[...] < float(min(SLC_TOPK, ns))) & causal
    parts = [jnp.where(sel, 0.0, NEG)]
    if BIAS_ROWS - ns > 0:
        parts.append(jnp.zeros((BIAS_ROWS - ns, tq), jnp.float32))
    selb_ref[0, 0] = jnp.concatenate(parts, axis=0).astype(jnp.bfloat16)


def _cmp_bias(S):
    n = np.arange(S // CMP_STRIDE)[:, None]
    pos = np.arange(S)[None, :]
    return jnp.asarray(np.where(n * CMP_STRIDE + (CMP_BLOCK - 1) <= pos, 0.0, NEG), jnp.float32)


def _cmp_select(qt, kc, vct, gt, ovt, cbias, B, S):
    T = B * S
    tq = CMP_TILE
    nq = S // tq
    nc = kc.shape[2]
    ns = S // SLC_BLOCK
    assert ns <= BIAS_ROWS and ns % SUBLANES == 0
    G = NSA_KV_GROUPS
    bf = jnp.bfloat16
    return pl.pallas_call(
        functools.partial(_cmp_select_kernel, tq=tq, nc=nc, ns=ns),
        grid=(B, G, nq),
        in_specs=[
            pl.BlockSpec((1, NSA_REP, LANES, tq), lambda b, g, i: (b, g, 0, i)),
            pl.BlockSpec((1, 1, nc, LANES), lambda b, g, i: (b, g, 0, 0)),
            pl.BlockSpec((1, 1, LANES, nc), lambda b, g, i: (b, G + g, 0, 0)),
            pl.BlockSpec((NSA_HEADS, tq), lambda b, g, i: (0, b * nq + i)),
            pl.BlockSpec((ns, nc), lambda b, g, i: (0, 0)),
            pl.BlockSpec((nc, tq), lambda b, g, i: (0, i)),
        ],
        out_specs=[pl.BlockSpec((tq, NSA_REP * NSA_HEAD_DIM), lambda b, g, i: (b * nq + i, g)),
                   pl.BlockSpec((1, 1, BIAS_ROWS, tq), lambda b, g, i: (b, g, 0, i))],
        out_shape=[jax.ShapeDtypeStruct((T, NSA_HEADS * NSA_HEAD_DIM), bf),
                   jax.ShapeDtypeStruct((B, G, BIAS_ROWS, S), bf)],
        scratch_shapes=[pltpu.VMEM((ns, tq), jnp.float32), pltpu.VMEM((ns, tq), jnp.float32)],
        compiler_params=pltpu.CompilerParams(dimension_semantics=("arbitrary",) * 3,
                                             vmem_limit_bytes=VMEM_LIMIT_BYTES),
        name="cmp_select",
    )(qt, kc, vct, gt, ovt, cbias)


def _flash_kernel(*refs, hpk, kv_per_step, tq, tk, window, has_sel, has_gate):
    refs = list(refs)
    qt_ref = refs.pop(0)
    selb_ref = refs.pop(0) if has_sel else None
    k_ref = refs.pop(0)
    vt_ref = refs.pop(0)
    bias_ref = refs.pop(0)
    gate_ref = refs.pop(0) if has_gate else None
    o_ref, q_scr, p_scr, m_scr, alpha_scr, acc_scr = refs
    nh = hpk * kv_per_step
    hd = NSA_HEAD_DIM
    j = pl.program_id(1)
    qi = pl.program_id(2)

    for h in range(nh):
        q = qt_ref[0, h]
        if has_sel:
            q = jnp.concatenate([q[:hd], selb_ref[0, 0]], axis=0)
        q_scr[h] = q
    m_scr[...] = jnp.full(m_scr.shape, NEG, jnp.float32)
    acc_scr[...] = jnp.zeros(acc_scr.shape, jnp.float32)

    def values(kt, h, slot):
        vtile = vt_ref[0, h // hpk, kt]
        acc_scr[h] = alpha_scr[slot, h] * acc_scr[h] + _dot(vtile, p_scr[slot, h])

    def tile_step(kt, bias_idx, prev, slot):
        s_all = []
        for p in range(kv_per_step):
            ktile = k_ref[pl.ds(pl.multiple_of(kt * tk, tk), tk), p * LANES:(p + 1) * LANES]
            for r in range(hpk):
                s = _dot(ktile, q_scr[p * hpk + r])
                if bias_idx is not None:
                    s = s + bias_ref[bias_idx]
                s_all.append(s)
        for h in range(nh):
            if prev is not None:
                values(prev, h, 1 - slot)
            s = s_all[h]
            m_prev = m_scr[h]
            m_new = jnp.maximum(m_prev, jnp.max(s, axis=0, keepdims=True))
            alpha_scr[slot, h] = jnp.exp2(m_prev - m_new)
            p_scr[slot, h] = jnp.exp2(s - m_new).astype(jnp.bfloat16)
            m_scr[h] = m_new

    tile_step(qi, 0, None, 0)
    if window is None:
        lo, n_extra, bias_idx = 0, qi, None
    else:
        lo, n_extra, bias_idx = qi - 1, jnp.minimum(qi, 1), 1

    def pair(i, carry):
        kt = lo + 2 * i
        tile_step(kt, bias_idx, jnp.where(i == 0, qi, kt - 1), 1)
        tile_step(kt + 1, bias_idx, kt, 0)
        return carry

    lax.fori_loop(0, n_extra // 2, pair, 0)
    last = lo + n_extra - 1

    @pl.when(n_extra % 2 == 1)
    def _():
        tile_step(last, bias_idx, jnp.where(n_extra == 1, qi, last - 1), 1)
        for h in range(nh):
            values(last, h, 1)

    @pl.when(n_extra % 2 == 0)
    def _():
        for h in range(nh):
            values(jnp.where(n_extra == 0, qi, last), h, 0)

    outs = []
    for h in range(nh):
        acc = acc_scr[h]
        o = acc[:hd] / jnp.maximum(acc[hd:hd + 1], TINY)
        if has_gate:
            o = o * jnp.where(j == 0, gate_ref[h:h + 1, :], gate_ref[nh + h:nh + h + 1, :])
        outs.append(o)
    o_ref[...] = jnp.concatenate(outs, axis=0).T.astype(jnp.bfloat16)


def _tile_biases(t):
    k = np.arange(t)[:, None]
    q = np.arange(t)[None, :]
    return jnp.asarray(np.stack([np.where(k <= q, 0.0, NEG), np.where(k > q, 0.0, NEG)]), jnp.float32)


def _flash(qt, k, vt, B, S, *, hpk, kv_per_step, window=None, selb=None, gate=None, gate_blk=0, name):
    T = B * S
    tq = tk = FLASH_TILE
    assert window is None or window == tk
    nq = S // tq
    nkv = vt.shape[1]
    nh = hpk * kv_per_step
    steps = nkv // kv_per_step
    hd = NSA_HEAD_DIM
    in_specs = [pl.BlockSpec((1, nh, LANES, tq), lambda b, j, i: (b, j, 0, i))]
    args = [qt]
    if selb is not None:
        in_specs.append(pl.BlockSpec((1, 1, BIAS_ROWS, tq), lambda b, j, i: (b, j, 0, i)))
        args.append(selb)
    in_specs.append(pl.BlockSpec((S, kv_per_step * LANES), lambda b, j, i: (b, j)))
    in_specs.append(pl.BlockSpec((1, kv_per_step, S // tk, LANES, tk), lambda b, j, i: (b, j, 0, 0, 0)))
    in_specs.append(pl.BlockSpec((2, tk, tq), lambda b, j, i: (0, 0, 0)))
    args += [k, vt, _tile_biases(tq)]
    if gate is not None:
        in_specs.append(pl.BlockSpec((NSA_HEADS, tq), lambda b, j, i: (gate_blk, b * nq + i)))
        args.append(gate)
    return pl.pallas_call(
        functools.partial(_flash_kernel, hpk=hpk, kv_per_step=kv_per_step, tq=tq, tk=tk, window=window,
                          has_sel=selb is not None, has_gate=gate is not None),
        grid=(B, steps, nq),
        in_specs=in_specs,
        out_specs=pl.BlockSpec((tq, nh * hd), lambda b, j, i: (b * nq + i, j)),
        out_shape=jax.ShapeDtypeStruct((T, nkv * hpk * hd), jnp.bfloat16),
        scratch_shapes=[pltpu.VMEM((nh, LANES, tq), jnp.bfloat16),
                        pltpu.VMEM((2, nh, tk, tq), jnp.bfloat16),
                        pltpu.VMEM((nh, 1, tq), jnp.float32),
                        pltpu.VMEM((2, nh, 1, tq), jnp.float32),
                        pltpu.VMEM((nh, LANES, tq), jnp.float32)],
        compiler_params=pltpu.CompilerParams(dimension_semantics=("arbitrary",) * 3,
                                             vmem_limit_bytes=VMEM_LIMIT_BYTES),
        name=name,
    )(*args)


def _merge_kernel(oc_ref, os_ref, ow_ref, ob_ref, ga_ref, gb_ref, x_ref, wa_ref, wb_ref, wo_ref, out_ref):
    f32 = jnp.float32
    o_a = (oc_ref[...].astype(f32) + os_ref[...].astype(f32) + ow_ref[...].astype(f32)).astype(jnp.bfloat16)
    merged = (ga_ref[...].astype(f32) * _dot(o_a, wa_ref[...])
              + gb_ref[...].astype(f32) * _dot(ob_ref[...], wb_ref[...]))
    out_ref[...] = x_ref[...] + _dot(merged.astype(jnp.bfloat16), wo_ref[...])


def _merge(o_cmp, o_slc, o_win, o_b, ga, gb, x2, lw):
    T = x2.shape[0]
    tm = TOKEN_TILE

    def rows(width):
        return pl.BlockSpec((tm, width), lambda i: (i, 0))

    def full(a):
        return pl.BlockSpec(a.shape, lambda i: (0, 0))

    w = NSA_HEADS * NSA_HEAD_DIM
    return pl.pallas_call(
        _merge_kernel,
        grid=(T // tm,),
        in_specs=[rows(w), rows(w), rows(w), rows(MLA_HEADS * MLA_V), rows(D_MODEL), rows(D_MODEL),
                  rows(D_MODEL), full(lw['nsa_w_o']), full(lw['mla_w_o']), full(lw['w_out'])],
        out_specs=rows(D_MODEL),
        out_shape=jax.ShapeDtypeStruct((T, D_MODEL), jnp.float32),
        compiler_params=pltpu.CompilerParams(dimension_semantics=("arbitrary",),
                                             vmem_limit_bytes=VMEM_LIMIT_BYTES),
        name="merge",
    )(o_cmp, o_slc, o_win, o_b, ga, gb, x2, lw['nsa_w_o'], lw['mla_w_o'], lw['w_out'])


def _mlp_kernel(x_ref, g_ref, wu_ref, wd_ref, gf_ref, out_ref, *, ff_chunk, final):
    x = x_ref[...]
    hn = _rms(x, g_ref[...]).astype(jnp.bfloat16)
    acc = jnp.zeros(x.shape, jnp.float32)
    for c in range(D_FF // ff_chunk):
        h = jnp.maximum(_dot(hn, wu_ref[:, c * ff_chunk:(c + 1) * ff_chunk]), 0.0)
        acc = acc + _dot((h * h).astype(jnp.bfloat16), wd_ref[c * ff_chunk:(c + 1) * ff_chunk, :])
    y = x + acc
    if final:
        y = _rms(y, gf_ref[...])
    out_ref[...] = y


def _mlp(x2, lw, final_norm, final):
    T = x2.shape[0]
    tm = TOKEN_TILE

    def full(a):
        return pl.BlockSpec(a.shape, lambda i: (0, 0))

    return pl.pallas_call(
        functools.partial(_mlp_kernel, ff_chunk=1024, final=final),
        grid=(T // tm,),
        in_specs=[pl.BlockSpec((tm, D_MODEL), lambda i: (i, 0)), full(lw['mlp_norm']),
                  full(lw['w_up']), full(lw['w_down']), full(final_norm)],
        out_specs=pl.BlockSpec((tm, D_MODEL), lambda i: (i, 0)),
        out_shape=jax.ShapeDtypeStruct((T, D_MODEL), jnp.float32),
        compiler_params=pltpu.CompilerParams(dimension_semantics=("arbitrary",),
                                             vmem_limit_bytes=VMEM_LIMIT_BYTES),
        name="mlp",
    )(x2, lw['mlp_norm'], lw['w_up'], lw['w_down'], final_norm)


def _pad_cols(w, width, left=0):
    return jnp.pad(w, ((0, 0), (left, width - left - w.shape[1])))


def _prep_layer(p, l):
    bf = jnp.bfloat16
    hd = NSA_HEAD_DIM
    G = NSA_KV_GROUPS
    w = p['w_in'][l]
    cols = [w[:, _OFF_KVC:_OFF_KVS]]
    for off in (_OFF_KVS, _OFF_KVW):
        for g in range(G):
            cols.append(_pad_cols(w[:, off + g * hd:off + (g + 1) * hd], LANES))
    cols.append(w[:, _OFF_CQ:_OFF_CKV])
    cols.append(w[:, _OFF_CKV:_OFF_KR])
    cols.append(_pad_cols(w[:, _OFF_KR:_OFF_GA], LANES, left=MLA_NOPE))
    cols.append(w[:, _OFF_GA:_OFF_GB])
    cols.append(w[:, _OFF_GB:_OFF_GB + D_MODEL])
    w_row = jnp.concatenate(cols, axis=1).astype(bf)

    t_cols = [w[:, _OFF_Q:_OFF_KVC]]
    t_cols += [w[:, off + (G + g) * hd:off + (G + g + 1) * hd] for off in (_OFF_KVS, _OFF_KVW) for g in range(G)]
    t_cols.append(_pad_cols(w[:, _OFF_NG:_OFF_CQ], GATE_ROWS))
    w_t = jnp.concatenate(t_cols, axis=1).T.astype(bf)

    wukv = p['mla_w_ukv'][l]
    dkv = MLA_NOPE + MLA_V
    w_k = jnp.concatenate([_pad_cols(wukv[:, h * dkv:h * dkv + MLA_NOPE], LANES) for h in range(MLA_HEADS)], axis=1)
    w_vt = jnp.concatenate([wukv[:, h * dkv + MLA_NOPE:(h + 1) * dkv] for h in range(MLA_HEADS)], axis=1).T

    cmp_w2 = p['cmp_w2'][l]
    cmp_w2p = jnp.pad(cmp_w2, ((0, 0), (0, 0), (0, LANES - hd)))
    return {
        'attn_norm': p['attn_norm'][l][None, :],
        'w_row': w_row, 'w_t': w_t,
        'q_norm': p['mla_q_norm'][l][None, :], 'kv_norm': p['mla_kv_norm'][l][None, :],
        'w_uqt': p['mla_w_uq'][l].T.astype(bf), 'w_vt': w_vt.astype(bf), 'w_k': w_k.astype(bf),
        'cmp_pe': p['cmp_pe'][l].reshape(2, 1, CMP_BLOCK * hd),
        'cmp_w1': p['cmp_w1'][l].astype(bf),
        'cmp_w2': cmp_w2p.astype(bf),
        'cmp_w2t': jnp.swapaxes(cmp_w2p, 1, 2).astype(bf),
        'nsa_w_o': p['nsa_w_o'][l].astype(bf), 'mla_w_o': p['mla_w_o'][l].astype(bf),
        'w_out': p['w_out'][l].astype(bf),
        'mlp_norm': p['mlp_norm'][l][None, :],
        'w_up': p['w_up'][l].astype(bf), 'w_down': p['w_down'][l].astype(bf),
    }


def _tables(S):
    pos = jnp.arange(S, dtype=jnp.int32).astype(jnp.float32)
    hd = NSA_HEAD_DIM

    def angles(d):
        inv = ROPE_THETA ** (-jnp.arange(0, d, 2, dtype=jnp.float32) / d)
        ang = pos[:, None] * inv[None, :]
        return jnp.cos(ang), jnp.sin(ang)

    cos, sin = angles(hd)
    z = jnp.zeros_like(cos)
    pad = jnp.zeros((S, LANES - hd), jnp.float32)
    cc = jnp.concatenate([cos, cos, cos, cos], axis=1)
    s1c = jnp.concatenate([z, sin, z, sin], axis=1)
    s2c = jnp.concatenate([-sin, z, -sin, z], axis=1)
    ck = jnp.concatenate([cos, cos, pad], axis=1)
    s1k = jnp.concatenate([z, sin, pad], axis=1)
    s2k = jnp.concatenate([-sin, z, pad], axis=1)
    q_scale = hd ** -0.5 * LOG2E

    cos16, sin16 = angles(MLA_ROPE)
    z16 = jnp.zeros_like(cos16)
    z64 = jnp.zeros((S, MLA_NOPE), jnp.float32)
    pad32 = jnp.zeros((S, LANES - MLA_NOPE - MLA_ROPE), jnp.float32)
    ckp = jnp.concatenate([z64, cos16, cos16, pad32], axis=1)
    s1kp = jnp.concatenate([z64, z16, sin16, pad32], axis=1)
    s2kp = jnp.concatenate([z64, -sin16, z16, pad32], axis=1)
    m_scale = (MLA_NOPE + MLA_ROPE) ** -0.5 * LOG2E
    return {'cc': cc, 's1c': s1c, 's2c': s2c, 'ck': ck, 's1k': s1k, 's2k': s2k,
            'ckp': ckp, 's1kp': s1kp, 's2kp': s2kp,
            'cosq_t': cos.T * q_scale, 'sinq_t': sin.T * q_scale,
            'cosm_t': cos16.T * m_scale, 'sinm_t': sin16.T * m_scale}


def _overlap_t(S):
    nc = S // CMP_STRIDE
    n_cmp = (S - CMP_BLOCK) // CMP_STRIDE + 1
    ns = S // SLC_BLOCK
    start = np.arange(nc) * CMP_STRIDE
    sel = np.arange(ns) * SLC_BLOCK
    lo = np.maximum(start[None, :], sel[:, None])
    hi = np.minimum(start[None, :] + CMP_BLOCK, sel[:, None] + SLC_BLOCK)
    ov = np.clip(hi - lo, 0, None) / CMP_BLOCK
    ov[:, n_cmp:] = 0.0
    return jnp.asarray(ov, jnp.bfloat16)


def _cmp_blocks(cmp_in, B, S):
    hd = NSA_HEAD_DIM
    nc = S // CMP_STRIDE
    r = cmp_in.reshape(B, nc, CMP_STRIDE, 2 * NSA_KV_GROUPS, hd)
    r = r.transpose(0, 3, 1, 2, 4).reshape(B, 2 * NSA_KV_GROUPS, nc, CMP_STRIDE * hd)
    nxt = jnp.concatenate([r[:, :, 1:], jnp.zeros_like(r[:, :, :1])], axis=2)
    return jnp.concatenate([r, nxt], axis=-1)


def _forward(x, params, depth):
    B, S, D = x.shape
    T = B * S
    tabs = _tables(S)
    ovt = _overlap_t(S)
    cbias = _cmp_bias(S)
    x2 = x.reshape(T, D)
    final_norm = params['final_norm'][None, :]
    for l in range(depth):
        lw = _prep_layer(params, l)
        (cmp_in, ks, kw, ga, gb, km, qt, vst, vwt, gt, qmt, vmt) = _in_proj(x2, lw, tabs, B, S)
        cmp_t, cmp_r = _compress(_cmp_blocks(cmp_in, B, S), lw)
        o_cmp, selb = _cmp_select(qt, cmp_r, cmp_t, gt, ovt, cbias, B, S)
        o_slc = _flash(qt, ks, vst, B, S, hpk=NSA_REP, kv_per_step=1, selb=selb,
                       gate=gt, gate_blk=1, name="flash_slc")
        o_win = _flash(qt, kw, vwt, B, S, hpk=NSA_REP, kv_per_step=1, window=WINDOW,
                       gate=gt, gate_blk=2, name="flash_win")
        o_b = _flash(qmt, km, vmt, B, S, hpk=1, kv_per_step=4, name="flash_mla")
        x2 = _merge(o_cmp, o_slc, o_win, o_b, ga, gb, x2, lw)
        x2 = _mlp(x2, lw, final_norm, final=(l == depth - 1))
    return x2.reshape(B, S, D)


def kernel(x, attn_norm, w_in, cmp_pe, cmp_w1, cmp_w2, nsa_w_o, mla_q_norm, mla_kv_norm, mla_w_uq,
           mla_w_ukv, mla_w_o, w_out, mlp_norm, w_up, w_down, final_norm):
    params = dict(attn_norm=attn_norm, w_in=w_in, cmp_pe=cmp_pe, cmp_w1=cmp_w1, cmp_w2=cmp_w2,
                  nsa_w_o=nsa_w_o, mla_q_norm=mla_q_norm, mla_kv_norm=mla_kv_norm, mla_w_uq=mla_w_uq,
                  mla_w_ukv=mla_w_ukv, mla_w_o=mla_w_o, w_out=w_out, mlp_norm=mlp_norm, w_up=w_up,
                  w_down=w_down, final_norm=final_norm)
    return _forward(x, params, w_in.shape[0])
```

```python
import functools

import numpy as np
import jax
import jax.numpy as jnp
from jax import lax
from jax.experimental import pallas as pl
from jax.experimental.pallas import tpu as pltpu

EPS = 1e-6
ROPE_THETA = 10000.0
NEG = -1e30
TINY = 1e-30
LOG2E = 1.4426950408889634

D_MODEL = 1024
NSA_HEADS = 8
NSA_KV_GROUPS = 2
NSA_REP = NSA_HEADS // NSA_KV_GROUPS
NSA_HEAD_DIM = 64
CMP_BLOCK = 32
CMP_STRIDE = 16
CMP_HIDDEN = 4 * NSA_HEAD_DIM
SLC_BLOCK = 64
SLC_TOPK = 16
N_LOCAL_FORCED = 2
FORCE_BONUS = 1e4
WINDOW = 512

MLA_HEADS = 8
MLA_Q_LORA = 256
MLA_KV_LORA = 128
MLA_NOPE = 64
MLA_ROPE = 32
MLA_V = 64
D_FF = 4 * D_MODEL

LANES = 128
SUBLANES = 8
VMEM_LIMIT_BYTES = 56 * 1024 * 1024

TOKEN_TILE = 512
FLASH_TILE = 512
CMP_TILE = 512
BIAS_ROWS = LANES - NSA_HEAD_DIM
GATE_ROWS = 32

_OFF_Q = 0
_OFF_KVC = _OFF_Q + NSA_HEADS * NSA_HEAD_DIM
_OFF_KVS = _OFF_KVC + 2 * NSA_KV_GROUPS * NSA_HEAD_DIM
_OFF_KVW = _OFF_KVS + 2 * NSA_KV_GROUPS * NSA_HEAD_DIM
_OFF_NG = _OFF_KVW + 2 * NSA_KV_GROUPS * NSA_HEAD_DIM
_OFF_CQ = _OFF_NG + 3 * NSA_HEADS
_OFF_CKV = _OFF_CQ + MLA_Q_LORA
_OFF_KR = _OFF_CKV + MLA_KV_LORA
_OFF_GA = _OFF_KR + MLA_ROPE
_OFF_GB = _OFF_GA + D_MODEL

_R_CMP = 0
_R_KS = _R_CMP + 2 * NSA_KV_GROUPS * NSA_HEAD_DIM
_R_KW = _R_KS + NSA_KV_GROUPS * LANES
_R_CQ = _R_KW + NSA_KV_GROUPS * LANES
_R_CKV = _R_CQ + MLA_Q_LORA
_R_KPE = _R_CKV + MLA_KV_LORA
_R_GA = _R_KPE + LANES
_R_GB = _R_GA + D_MODEL
_R_END = _R_GB + D_MODEL

_T_Q = 0
_T_VS = _T_Q + NSA_HEADS * NSA_HEAD_DIM
_T_VW = _T_VS + NSA_KV_GROUPS * NSA_HEAD_DIM
_T_NG = _T_VW + NSA_KV_GROUPS * NSA_HEAD_DIM
_T_END = _T_NG + GATE_ROWS


def _nt_dot(a, b):
    return lax.dot_general(a, b, (((1,), (1,)), ((), ())), preferred_element_type=jnp.float32)


def _dot(a, b):
    return jnp.dot(a, b, preferred_element_type=jnp.float32)


def _rms(xf, g):
    return xf * lax.rsqrt(jnp.mean(xf * xf, axis=-1, keepdims=True) + EPS) * g


def _rope_rows(y, c, s1, s2, sh):
    return y * c + pltpu.roll(y, sh, 1) * s1 + pltpu.roll(y, LANES - sh, 1) * s2


def _rope_cols(y, cos_t, sin_t):
    h = y.shape[0] // 2
    a, b = y[:h], y[h:]
    return jnp.concatenate([a * cos_t - b * sin_t, a * sin_t + b * cos_t], axis=0)


def _in_proj_kernel(x_ref, g_ref, w_ref, wt_ref, qng_ref, kvng_ref, wuqt_ref, wvt_ref, wk_ref,
                    cc_ref, s1c_ref, s2c_ref, ck_ref, s1k_ref, s2k_ref, ckp_ref, s1kp_ref, s2kp_ref,
                    cosq_ref, sinq_ref, cosm_ref, sinm_ref,
                    cmp_ref, ks_ref, kw_ref, ga_ref, gb_ref, km_ref,
                    qt_ref, vst_ref, vwt_ref, gt_ref, qmt_ref, vmt_ref, *, tm, tk, s_tiles):
    si = pl.program_id(0) % s_tiles
    hd = NSA_HEAD_DIM
    bf = jnp.bfloat16
    xn = _rms(x_ref[...], g_ref[...]).astype(bf)

    y = _dot(xn, w_ref[:, _R_CMP:_R_KS])
    cmp_ref[:, :LANES] = _rope_rows(y[:, :LANES], cc_ref[...], s1c_ref[...], s2c_ref[...], hd // 2)
    cmp_ref[:, LANES:] = y[:, LANES:]

    lane = lax.broadcasted_iota(jnp.int32, (tm, LANES), 1)
    pos = si * tm + lax.broadcasted_iota(jnp.int32, (tm, LANES), 0)
    sel_cols = jnp.where(lane - hd == pos // SLC_BLOCK, 1.0, 0.0)
    ck, s1k, s2k = ck_ref[...], s1k_ref[...], s2k_ref[...]
    y = _dot(xn, w_ref[:, _R_KS:_R_KW])
    for g in range(NSA_KV_GROUPS):
        blk = _rope_rows(y[:, g * LANES:(g + 1) * LANES], ck, s1k, s2k, hd // 2)
        ks_ref[:, g * LANES:(g + 1) * LANES] = (blk + sel_cols).astype(bf)
    y = _dot(xn, w_ref[:, _R_KW:_R_CQ])
    for g in range(NSA_KV_GROUPS):
        kw_ref[:, g * LANES:(g + 1) * LANES] = _rope_rows(y[:, g * LANES:(g + 1) * LANES], ck, s1k, s2k,
                                                          hd // 2).astype(bf)

    ga_ref[...] = jax.nn.sigmoid(_dot(xn, w_ref[:, _R_GA:_R_GB])).astype(bf)
    gb_ref[...] = jax.nn.sigmoid(_dot(xn, w_ref[:, _R_GB:_R_END])).astype(bf)

    cqn = _rms(_dot(xn, w_ref[:, _R_CQ:_R_CKV]), qng_ref[...]).astype(bf)
    ckvn = _rms(_dot(xn, w_ref[:, _R_CKV:_R_KPE]), kvng_ref[...]).astype(bf)
    kpe = _rope_rows(_dot(xn, w_ref[:, _R_KPE:_R_GA]), ckp_ref[...], s1kp_ref[...], s2kp_ref[...], MLA_ROPE // 2)
    y = _dot(ckvn, wk_ref[...])
    for h in range(MLA_HEADS):
        km_ref[:, h * LANES:(h + 1) * LANES] = (y[:, h * LANES:(h + 1) * LANES] + kpe).astype(bf)

    yt = _nt_dot(wt_ref[...], xn)
    cosq, sinq = cosq_ref[...], sinq_ref[...]
    zero_rows = jnp.zeros((LANES - hd, tm), jnp.float32)
    for h in range(NSA_HEADS):
        q = _rope_cols(yt[_T_Q + h * hd:_T_Q + (h + 1) * hd], cosq, sinq)
        qt_ref[0, h] = jnp.concatenate([q, zero_rows], axis=0).astype(bf)
    ones_rows = jnp.where(lax.broadcasted_iota(jnp.int32, (LANES - hd, tm), 0) == 0, 1.0, 0.0)
    for g in range(NSA_KV_GROUPS):
        vs = jnp.concatenate([yt[_T_VS + g * hd:_T_VS + (g + 1) * hd], ones_rows], axis=0).astype(bf)
        vw = jnp.concatenate([yt[_T_VW + g * hd:_T_VW + (g + 1) * hd], ones_rows], axis=0).astype(bf)
        for c in range(tm // tk):
            vst_ref[0, g, c] = vs[:, c * tk:(c + 1) * tk]
            vwt_ref[0, g, c] = vw[:, c * tk:(c + 1) * tk]
    gt_ref[...] = jax.nn.sigmoid(yt[_T_NG:_T_END])

    dq = MLA_NOPE + MLA_ROPE
    sm = dq ** -0.5 * LOG2E
    ymt = _nt_dot(wuqt_ref[...], cqn)
    cosm, sinm = cosm_ref[...], sinm_ref[...]
    pad_rows = jnp.zeros((LANES - dq, tm), jnp.float32)
    for h in range(MLA_HEADS):
        nope = ymt[h * dq:h * dq + MLA_NOPE] * sm
        pe = _rope_cols(ymt[h * dq + MLA_NOPE:(h + 1) * dq], cosm, sinm)
        qmt_ref[0, h] = jnp.concatenate([nope, pe, pad_rows], axis=0).astype(bf)
    yvt = _nt_dot(wvt_ref[...], ckvn)
    for h in range(MLA_HEADS):
        vm = jnp.concatenate([yvt[h * MLA_V:(h + 1) * MLA_V], ones_rows], axis=0).astype(bf)
        for c in range(tm // tk):
            vmt_ref[0, h, c] = vm[:, c * tk:(c + 1) * tk]


def _in_proj(x2, lw, tabs, B, S):
    T = B * S
    tm, tk = TOKEN_TILE, FLASH_TILE
    s_tiles = S // tm
    nkt = S // tk
    bf = jnp.bfloat16

    def full(a):
        return pl.BlockSpec(a.shape, lambda i, _n=a.ndim: (0,) * _n)

    def rows(width):
        return pl.BlockSpec((tm, width), lambda i: (i, 0))

    def tab_rows(a):
        return pl.BlockSpec((tm, a.shape[1]), lambda i: (i % s_tiles, 0))

    def tab_cols(a):
        return pl.BlockSpec((a.shape[0], tm), lambda i: (0, i % s_tiles))

    def qt_spec(n):
        return pl.BlockSpec((1, n, LANES, tm), lambda i: (i // s_tiles, 0, 0, i % s_tiles))

    def vt_spec(n):
        return pl.BlockSpec((1, n, tm // tk, LANES, tk), lambda i: (i // s_tiles, 0, i % s_tiles, 0, 0))

    weights = [lw['attn_norm'], lw['w_row'], lw['w_t'], lw['q_norm'], lw['kv_norm'],
               lw['w_uqt'], lw['w_vt'], lw['w_k']]
    row_tabs = [tabs[k] for k in ('cc', 's1c', 's2c', 'ck', 's1k', 's2k', 'ckp', 's1kp', 's2kp')]
    col_tabs = [tabs[k] for k in ('cosq_t', 'sinq_t', 'cosm_t', 'sinm_t')]
    in_specs = ([rows(D_MODEL)] + [full(w) for w in weights] + [tab_rows(t) for t in row_tabs]
                + [tab_cols(t) for t in col_tabs])
    G, H = NSA_KV_GROUPS, MLA_HEADS
    out_shape = [
        jax.ShapeDtypeStruct((T, 2 * G * NSA_HEAD_DIM), jnp.float32),
        jax.ShapeDtypeStruct((T, G * LANES), bf),
        jax.ShapeDtypeStruct((T, G * LANES), bf),
        jax.ShapeDtypeStruct((T, D_MODEL), bf),
        jax.ShapeDtypeStruct((T, D_MODEL), bf),
        jax.ShapeDtypeStruct((T, H * LANES), bf),
        jax.ShapeDtypeStruct((B, NSA_HEADS, LANES, S), bf),
        jax.ShapeDtypeStruct((B, G, nkt, LANES, tk), bf),
        jax.ShapeDtypeStruct((B, G, nkt, LANES, tk), bf),
        jax.ShapeDtypeStruct((GATE_ROWS, T), jnp.float32),
        jax.ShapeDtypeStruct((B, H, LANES, S), bf),
        jax.ShapeDtypeStruct((B, H, nkt, LANES, tk), bf),
    ]
    out_specs = [rows(2 * G * NSA_HEAD_DIM), rows(G * LANES), rows(G * LANES), rows(D_MODEL), rows(D_MODEL),
                 rows(H * LANES), qt_spec(NSA_HEADS), vt_spec(G), vt_spec(G),
                 pl.BlockSpec((GATE_ROWS, tm), lambda i: (0, i)), qt_spec(H), vt_spec(H)]
    return pl.pallas_call(
        functools.partial(_in_proj_kernel, tm=tm, tk=tk, s_tiles=s_tiles),
        grid=(T // tm,),
        in_specs=in_specs,
        out_specs=out_specs,
        out_shape=out_shape,
        compiler_params=pltpu.CompilerParams(dimension_semantics=("arbitrary",),
                                             vmem_limit_bytes=VMEM_LIMIT_BYTES),
        name="in_proj",
    )(x2, *weights, *row_tabs, *col_tabs)


def _compress_kernel(x_ref, pe_ref, w1_ref, w2_ref, w2t_ref, ot_ref, or_ref, xs_scr, *, seq, nc):
    G = NSA_KV_GROUPS
    bf = jnp.bfloat16
    for kv in range(2):
        xs_scr[kv, pl.ds(0, seq), :] = x_ref[:, kv * LANES:(kv + 1) * LANES]
        xs_scr[kv, pl.ds(seq, CMP_STRIDE), :] = jnp.zeros((CMP_STRIDE, LANES), jnp.float32)
    acc = [jnp.zeros((nc, CMP_HIDDEN), jnp.float32) for _ in range(2 * G)]
    for l in range(CMP_BLOCK):
        for kv in range(2):
            rows = xs_scr[kv, pl.ds(l, nc, stride=CMP_STRIDE), :] + pe_ref[l, :, kv * LANES:(kv + 1) * LANES]
            rows = rows.astype(bf)
            for g in range(G):
                acc[kv * G + g] = acc[kv * G + g] + _dot(rows, w1_ref[kv, g, l])
    for c in range(2 * G):
        h = acc[c]
        h = (h * jax.nn.sigmoid(h)).astype(bf)
        ot_ref[0, c] = _nt_dot(w2t_ref[c // G], h).astype(bf)
        or_ref[0, c] = _dot(h, w2_ref[c // G]).astype(bf)


def _compress(cmp_in, lw, B, S):
    nc = S // CMP_STRIDE
    bf = jnp.bfloat16
    width = cmp_in.shape[1]

    def resident(a):
        return pl.BlockSpec(a.shape, lambda b, _n=a.ndim: (0,) * _n, pipeline_mode=pl.Buffered(1))

    weights = [lw['cmp_pe'], lw['cmp_w1'], lw['cmp_w2'], lw['cmp_w2t']]
    return pl.pallas_call(
        functools.partial(_compress_kernel, seq=S, nc=nc),
        grid=(B,),
        in_specs=[pl.BlockSpec((S, width), lambda b: (b, 0))] + [resident(a) for a in weights],
        out_specs=[pl.BlockSpec((1, 2 * NSA_KV_GROUPS, LANES, nc), lambda b: (b, 0, 0, 0)),
                   pl.BlockSpec((1, 2 * NSA_KV_GROUPS, nc, LANES), lambda b: (b, 0, 0, 0))],
        out_shape=[jax.ShapeDtypeStruct((B, 2 * NSA_KV_GROUPS, LANES, nc), bf),
                   jax.ShapeDtypeStruct((B, 2 * NSA_KV_GROUPS, nc, LANES), bf)],
        scratch_shapes=[pltpu.VMEM((2, S + CMP_STRIDE, LANES), jnp.float32)],
        compiler_params=pltpu.CompilerParams(dimension_semantics=("arbitrary",),
                                             vmem_limit_bytes=VMEM_LIMIT_BYTES),
        name="compress",
    )(cmp_in, *weights)


def _cmp_select_kernel(qt_ref, kc_ref, vct_ref, gate_ref, ovt_ref, cbias_ref, o_ref, selb_ref,
                       score_scr, cnt_scr, *, tq, nc, ns):
    g = pl.program_id(1)
    qi = pl.program_id(2)
    hd = NSA_HEAD_DIM
    kc = kc_ref[0, 0]
    vct = vct_ref[0, 0]
    cbias = cbias_ref[...]
    qpos = qi * tq + lax.broadcasted_iota(jnp.int32, (1, tq), 1)
    any_valid = jnp.where(qpos >= CMP_BLOCK - 1, 1.0, 0.0)

    psum = jnp.zeros((nc, tq), jnp.float32)
    outs = []
    for r in range(NSA_REP):
        s = _dot(kc, qt_ref[0, r]) + cbias
        e = jnp.exp2(s - jnp.max(s, axis=0, keepdims=True))
        p = e * (any_valid / jnp.maximum(jnp.sum(e, axis=0, keepdims=True), TINY))
        psum = psum + p
        o = _dot(vct, p.astype(jnp.bfloat16))
        gate = jnp.where(g == 0, gate_ref[r:r + 1, :], gate_ref[NSA_REP + r:NSA_REP + r + 1, :])
        outs.append(o[:hd] * gate)
    o_ref[...] = jnp.concatenate(outs, axis=0).T.astype(jnp.bfloat16)

    hi = psum.astype(jnp.bfloat16)
    lo = (psum - hi.astype(jnp.float32)).astype(jnp.bfloat16)
    ovt = ovt_ref[...]
    imp = _dot(ovt, hi) + _dot(ovt, lo)
    post = qi * tq + lax.broadcasted_iota(jnp.int32, (ns, tq), 1)
    blk = lax.broadcasted_iota(jnp.int32, (ns, tq), 0)
    cur = post // SLC_BLOCK
    causal = blk <= cur
    forced = (blk == 0) | ((cur - blk >= 0) & (cur - blk < N_LOCAL_FORCED))
    score_scr[...] = jnp.where(causal, imp + jnp.where(forced, FORCE_BONUS, 0.0), NEG)

    n_act = (tq // SLC_BLOCK * (qi + 1) + SUBLANES - 1) // SUBLANES
    sub_id = lax.broadcasted_iota(jnp.int32, (SUBLANES, tq), 0)
    for rg in range(ns // SUBLANES):
        rows = pl.ds(rg * SUBLANES, SUBLANES)

        @pl.when(rg < n_act)
        def _():
            sub = score_scr[rows, :]
            c = jnp.zeros((SUBLANES, tq), jnp.float32)
            for i in range((rg + 1) * SUBLANES):
                bi = score_scr[i:i + 1, :]
                if i < rg * SUBLANES:
                    ahead = bi >= sub
                else:
                    ahead = (bi > sub) | ((bi == sub) & (sub_id > (i - rg * SUBLANES)))
                c = c + jnp.where(ahead, 1.0, 0.0)

            def later_group(ig, c):
                for ii in range(SUBLANES):
                    bi = score_scr[pl.ds(ig * SUBLANES + ii, 1), :]
                    c = c + jnp.where(bi > sub, 1.0, 0.0)
                return c

            cnt_scr[rows, :] = lax.fori_loop(rg + 1, n_act, later_group, c)

        @pl.when(rg >= n_act)
        def _():
            cnt_scr[rows, :] = jnp.full((SUBLANES, tq), float(ns), jnp.float32)

    sel = (cnt_scr[...] < float(min(SLC_TOPK, ns))) & causal
    parts = [jnp.where(sel, 0.0, NEG)]
    if BIAS_ROWS - ns > 0:
        parts.append(jnp.zeros((BIAS_ROWS - ns, tq), jnp.float32))
    selb_ref[0, 0] = jnp.concatenate(parts, axis=0).astype(jnp.bfloat16)


def _cmp_bias(S):
    n = np.arange(S // CMP_STRIDE)[:, None]
    pos = np.arange(S)[None, :]
    return jnp.asarray(np.where(n * CMP_STRIDE + (CMP_BLOCK - 1) <= pos, 0.0, NEG), jnp.float32)


def _cmp_select(qt, kc, vct, gt, ovt, cbias, B, S):
    T = B * S
    tq = CMP_TILE
    nq = S // tq
    nc = kc.shape[2]
    ns = S // SLC_BLOCK
    assert ns <= BIAS_ROWS and ns % SUBLANES == 0
    G = NSA_KV_GROUPS
    bf = jnp.bfloat16
    return pl.pallas_call(
        functools.partial(_cmp_select_kernel, tq=tq, nc=nc, ns=ns),
        grid=(B, G, nq),
        in_specs=[
            pl.BlockSpec((1, NSA_REP, LANES, tq), lambda b, g, i: (b, g, 0, i)),
            pl.BlockSpec((1, 1, nc, LANES), lambda b, g, i: (b, g, 0, 0)),
            pl.BlockSpec((1, 1, LANES, nc), lambda b, g, i: (b, G + g, 0, 0)),
            pl.BlockSpec((NSA_HEADS, tq), lambda b, g, i: (0, b * nq + i)),
            pl.BlockSpec((ns, nc), lambda b, g, i: (0, 0)),
            pl.BlockSpec((nc, tq), lambda b, g, i: (0, i)),
        ],
        out_specs=[pl.BlockSpec((tq, NSA_REP * NSA_HEAD_DIM), lambda b, g, i: (b * nq + i, g)),
                   pl.BlockSpec((1, 1, BIAS_ROWS, tq), lambda b, g, i: (b, g, 0, i))],
        out_shape=[jax.ShapeDtypeStruct((T, NSA_HEADS * NSA_HEAD_DIM), bf),
                   jax.ShapeDtypeStruct((B, G, BIAS_ROWS, S), bf)],
        scratch_shapes=[pltpu.VMEM((ns, tq), jnp.float32), pltpu.VMEM((ns, tq), jnp.float32)],
        compiler_params=pltpu.CompilerParams(dimension_semantics=("arbitrary",) * 3,
                                             vmem_limit_bytes=VMEM_LIMIT_BYTES),
        name="cmp_select",
    )(qt, kc, vct, gt, ovt, cbias)


def _flash_kernel(*refs, hpk, kv_per_step, tq, tk, window, has_sel, has_gate):
    refs = list(refs)
    qt_ref = refs.pop(0)
    selb_ref = refs.pop(0) if has_sel else None
    k_ref = refs.pop(0)
    vt_ref = refs.pop(0)
    bias_ref = refs.pop(0)
    gate_ref = refs.pop(0) if has_gate else None
    o_ref, q_scr, p_scr, m_scr, alpha_scr, acc_scr = refs
    nh = hpk * kv_per_step
    hd = NSA_HEAD_DIM
    j = pl.program_id(1)
    qi = pl.program_id(2)

    for h in range(nh):
        q = qt_ref[0, h]
        if has_sel:
            q = jnp.concatenate([q[:hd], selb_ref[0, 0]], axis=0)
        q_scr[h] = q
    m_scr[...] = jnp.full(m_scr.shape, NEG, jnp.float32)
    acc_scr[...] = jnp.zeros(acc_scr.shape, jnp.float32)

    def values(kt, h, slot):
        vtile = vt_ref[0, h // hpk, kt]
        acc_scr[h] = alpha_scr[slot, h] * acc_scr[h] + _dot(vtile, p_scr[slot, h])

    def tile_step(kt, bias_idx, prev, slot):
        s_all = []
        for p in range(kv_per_step):
            ktile = k_ref[pl.ds(pl.multiple_of(kt * tk, tk), tk), p * LANES:(p + 1) * LANES]
            for r in range(hpk):
                s = _dot(ktile, q_scr[p * hpk + r])
                if bias_idx is not None:
                    s = s + bias_ref[bias_idx]
                s_all.append(s)
        for h in range(nh):
            if prev is not None:
                values(prev, h, 1 - slot)
            s = s_all[h]
            m_prev = m_scr[h]
            m_new = jnp.maximum(m_prev, jnp.max(s, axis=0, keepdims=True))
            alpha_scr[slot, h] = jnp.exp2(m_prev - m_new)
            p_scr[slot, h] = jnp.exp2(s - m_new).astype(jnp.bfloat16)
            m_scr[h] = m_new

    tile_step(qi, 0, None, 0)
    if window is None:
        lo, n_extra, bias_idx = 0, qi, None
    else:
        lo, n_extra, bias_idx = qi - 1, jnp.minimum(qi, 1), 1

    def pair(i, carry):
        kt = lo + 2 * i
        tile_step(kt, bias_idx, jnp.where(i == 0, qi, kt - 1), 1)
        tile_step(kt + 1, bias_idx, kt, 0)
        return carry

    lax.fori_loop(0, n_extra // 2, pair, 0)
    last = lo + n_extra - 1

    @pl.when(n_extra % 2 == 1)
    def _():
        tile_step(last, bias_idx, jnp.where(n_extra == 1, qi, last - 1), 1)
        for h in range(nh):
            values(last, h, 1)

    @pl.when(n_extra % 2 == 0)
    def _():
        for h in range(nh):
            values(jnp.where(n_extra == 0, qi, last), h, 0)

    outs = []
    for h in range(nh):
        acc = acc_scr[h]
        o = acc[:hd] / jnp.maximum(acc[hd:hd + 1], TINY)
        if has_gate:
            o = o * jnp.where(j == 0, gate_ref[h:h + 1, :], gate_ref[nh + h:nh + h + 1, :])
        outs.append(o)
    o_ref[...] = jnp.concatenate(outs, axis=0).T.astype(jnp.bfloat16)


def _tile_biases(t):
    k = np.arange(t)[:, None]
    q = np.arange(t)[None, :]
    return jnp.asarray(np.stack([np.where(k <= q, 0.0, NEG), np.where(k > q, 0.0, NEG)]), jnp.float32)


def _flash(qt, k, vt, B, S, *, hpk, kv_per_step, window=None, selb=None, gate=None, gate_blk=0, name):
    T = B * S
    tq = tk = FLASH_TILE
    assert window is None or window == tk
    nq = S // tq
    nkv = vt.shape[1]
    nh = hpk * kv_per_step
    steps = nkv // kv_per_step
    hd = NSA_HEAD_DIM
    in_specs = [pl.BlockSpec((1, nh, LANES, tq), lambda b, j, i: (b, j, 0, i))]
    args = [qt]
    if selb is not None:
        in_specs.append(pl.BlockSpec((1, 1, BIAS_ROWS, tq), lambda b, j, i: (b, j, 0, i)))
        args.append(selb)
    in_specs.append(pl.BlockSpec((S, kv_per_step * LANES), lambda b, j, i: (b, j)))
    in_specs.append(pl.BlockSpec((1, kv_per_step, S // tk, LANES, tk), lambda b, j, i: (b, j, 0, 0, 0)))
    in_specs.append(pl.BlockSpec((2, tk, tq), lambda b, j, i: (0, 0, 0)))
    args += [k, vt, _tile_biases(tq)]
    if gate is not None:
        in_specs.append(pl.BlockSpec((NSA_HEADS, tq), lambda b, j, i: (gate_blk, b * nq + i)))
        args.append(gate)
    return pl.pallas_call(
        functools.partial(_flash_kernel, hpk=hpk, kv_per_step=kv_per_step, tq=tq, tk=tk, window=window,
                          has_sel=selb is not None, has_gate=gate is not None),
        grid=(B, steps, nq),
        in_specs=in_specs,
        out_specs=pl.BlockSpec((tq, nh * hd), lambda b, j, i: (b * nq + i, j)),
        out_shape=jax.ShapeDtypeStruct((T, nkv * hpk * hd), jnp.bfloat16),
        scratch_shapes=[pltpu.VMEM((nh, LANES, tq), jnp.bfloat16),
                        pltpu.VMEM((2, nh, tk, tq), jnp.bfloat16),
                        pltpu.VMEM((nh, 1, tq), jnp.float32),
                        pltpu.VMEM((2, nh, 1, tq), jnp.float32),
                        pltpu.VMEM((nh, LANES, tq), jnp.float32)],
        compiler_params=pltpu.CompilerParams(dimension_semantics=("arbitrary",) * 3,
                                             vmem_limit_bytes=VMEM_LIMIT_BYTES),
        name=name,
    )(*args)


def _merge_mlp_kernel(oc_ref, os_ref, ow_ref, ob_ref, ga_ref, gb_ref, x_ref, wa_ref, wb_ref, wo_ref,
                      g_ref, wu_ref, wd_ref, gf_ref, out_ref, *, ff_chunk, final):
    f32 = jnp.float32
    bf = jnp.bfloat16
    o_a = (oc_ref[...].astype(f32) + os_ref[...].astype(f32) + ow_ref[...].astype(f32)).astype(bf)
    merged = (ga_ref[...].astype(f32) * _dot(o_a, wa_ref[...])
              + gb_ref[...].astype(f32) * _dot(ob_ref[...], wb_ref[...]))
    x = x_ref[...] + _dot(merged.astype(bf), wo_ref[...])
    hn = _rms(x, g_ref[...]).astype(bf)
    acc = jnp.zeros(x.shape, f32)
    for c in range(D_FF // ff_chunk):
        h = jnp.maximum(_dot(hn, wu_ref[:, c * ff_chunk:(c + 1) * ff_chunk]), 0.0)
        acc = acc + _dot((h * h).astype(bf), wd_ref[c * ff_chunk:(c + 1) * ff_chunk, :])
    y = x + acc
    if final:
        y = _rms(y, gf_ref[...])
    out_ref[...] = y


def _merge_mlp(o_cmp, o_slc, o_win, o_b, ga, gb, x2, lw, final_norm, final):
    T = x2.shape[0]
    tm = TOKEN_TILE

    def rows(width):
        return pl.BlockSpec((tm, width), lambda i: (i, 0))

    def resident(a):
        return pl.BlockSpec(a.shape, lambda i: (0, 0), pipeline_mode=pl.Buffered(1))

    w = NSA_HEADS * NSA_HEAD_DIM
    weights = [lw['nsa_w_o'], lw['mla_w_o'], lw['w_out'], lw['mlp_norm'], lw['w_up'], lw['w_down'], final_norm]
    return pl.pallas_call(
        functools.partial(_merge_mlp_kernel, ff_chunk=1024, final=final),
        grid=(T // tm,),
        in_specs=[rows(w), rows(w), rows(w), rows(MLA_HEADS * MLA_V), rows(D_MODEL), rows(D_MODEL),
                  rows(D_MODEL)] + [resident(a) for a in weights],
        out_specs=rows(D_MODEL),
        out_shape=jax.ShapeDtypeStruct((T, D_MODEL), jnp.float32),
        compiler_params=pltpu.CompilerParams(dimension_semantics=("arbitrary",),
                                             vmem_limit_bytes=VMEM_LIMIT_BYTES),
        name="merge_mlp",
    )(o_cmp, o_slc, o_win, o_b, ga, gb, x2, *weights)


def _pad_cols(w, width, left=0):
    return jnp.pad(w, ((0, 0), (left, width - left - w.shape[1])))


def _prep_layer(p, l):
    bf = jnp.bfloat16
    hd = NSA_HEAD_DIM
    G = NSA_KV_GROUPS
    w = p['w_in'][l]
    cols = [w[:, _OFF_KVC:_OFF_KVS]]
    for off in (_OFF_KVS, _OFF_KVW):
        for g in range(G):
            cols.append(_pad_cols(w[:, off + g * hd:off + (g + 1) * hd], LANES))
    cols.append(w[:, _OFF_CQ:_OFF_CKV])
    cols.append(w[:, _OFF_CKV:_OFF_KR])
    cols.append(_pad_cols(w[:, _OFF_KR:_OFF_GA], LANES, left=MLA_NOPE))
    cols.append(w[:, _OFF_GA:_OFF_GB])
    cols.append(w[:, _OFF_GB:_OFF_GB + D_MODEL])
    w_row = jnp.concatenate(cols, axis=1).astype(bf)

    t_cols = [w[:, _OFF_Q:_OFF_KVC]]
    t_cols += [w[:, off + (G + g) * hd:off + (G + g + 1) * hd] for off in (_OFF_KVS, _OFF_KVW) for g in range(G)]
    t_cols.append(_pad_cols(w[:, _OFF_NG:_OFF_CQ], GATE_ROWS))
    w_t = jnp.concatenate(t_cols, axis=1).T.astype(bf)

    wukv = p['mla_w_ukv'][l]
    dkv = MLA_NOPE + MLA_V
    w_k = jnp.concatenate([_pad_cols(wukv[:, h * dkv:h * dkv + MLA_NOPE], LANES) for h in range(MLA_HEADS)], axis=1)
    w_vt = jnp.concatenate([wukv[:, h * dkv + MLA_NOPE:(h + 1) * dkv] for h in range(MLA_HEADS)], axis=1).T

    assert G * hd == LANES
    pe = p['cmp_pe'][l]
    cmp_pe = jnp.concatenate([pe[0]] * G + [pe[1]] * G, axis=1)[:, None, :]
    w1 = p['cmp_w1'][l].reshape(2, CMP_BLOCK, hd, CMP_HIDDEN)
    cmp_w1 = jnp.stack([jnp.pad(w1, ((0, 0), (0, 0), (g * hd, LANES - (g + 1) * hd), (0, 0)))
                        for g in range(G)], axis=1).astype(bf)
    cmp_w2 = p['cmp_w2'][l]
    cmp_w2p = jnp.pad(cmp_w2, ((0, 0), (0, 0), (0, LANES - hd)))
    return {
        'attn_norm': p['attn_norm'][l][None, :],
        'w_row': w_row, 'w_t': w_t,
        'q_norm': p['mla_q_norm'][l][None, :], 'kv_norm': p['mla_kv_norm'][l][None, :],
        'w_uqt': p['mla_w_uq'][l].T.astype(bf), 'w_vt': w_vt.astype(bf), 'w_k': w_k.astype(bf),
        'cmp_pe': cmp_pe, 'cmp_w1': cmp_w1,
        'cmp_w2': cmp_w2p.astype(bf),
        'cmp_w2t': jnp.swapaxes(cmp_w2p, 1, 2).astype(bf),
        'nsa_w_o': p['nsa_w_o'][l].astype(bf), 'mla_w_o': p['mla_w_o'][l].astype(bf),
        'w_out': p['w_out'][l].astype(bf),
        'mlp_norm': p['mlp_norm'][l][None, :],
        'w_up': p['w_up'][l].astype(bf), 'w_down': p['w_down'][l].astype(bf),
    }


def _tables(S):
    pos = jnp.arange(S, dtype=jnp.int32).astype(jnp.float32)
    hd = NSA_HEAD_DIM

    def angles(d):
        inv = ROPE_THETA ** (-jnp.arange(0, d, 2, dtype=jnp.float32) / d)
        ang = pos[:, None] * inv[None, :]
        return jnp.cos(ang), jnp.sin(ang)

    cos, sin = angles(hd)
    z = jnp.zeros_like(cos)
    pad = jnp.zeros((S, LANES - hd), jnp.float32)
    cc = jnp.concatenate([cos, cos, cos, cos], axis=1)
    s1c = jnp.concatenate([z, sin, z, sin], axis=1)
    s2c = jnp.concatenate([-sin, z, -sin, z], axis=1)
    ck = jnp.concatenate([cos, cos, pad], axis=1)
    s1k = jnp.concatenate([z, sin, pad], axis=1)
    s2k = jnp.concatenate([-sin, z, pad], axis=1)
    q_scale = hd ** -0.5 * LOG2E

    cos16, sin16 = angles(MLA_ROPE)
    z16 = jnp.zeros_like(cos16)
    z64 = jnp.zeros((S, MLA_NOPE), jnp.float32)
    pad32 = jnp.zeros((S, LANES - MLA_NOPE - MLA_ROPE), jnp.float32)
    ckp = jnp.concatenate([z64, cos16, cos16, pad32], axis=1)
    s1kp = jnp.concatenate([z64, z16, sin16, pad32], axis=1)
    s2kp = jnp.concatenate([z64, -sin16, z16, pad32], axis=1)
    m_scale = (MLA_NOPE + MLA_ROPE) ** -0.5 * LOG2E
    return {'cc': cc, 's1c': s1c, 's2c': s2c, 'ck': ck, 's1k': s1k, 's2k': s2k,
            'ckp': ckp, 's1kp': s1kp, 's2kp': s2kp,
            'cosq_t': cos.T * q_scale, 'sinq_t': sin.T * q_scale,
            'cosm_t': cos16.T * m_scale, 'sinm_t': sin16.T * m_scale}


def _overlap_t(S):
    nc = S // CMP_STRIDE
    n_cmp = (S - CMP_BLOCK) // CMP_STRIDE + 1
    ns = S // SLC_BLOCK
    start = np.arange(nc) * CMP_STRIDE
    sel = np.arange(ns) * SLC_BLOCK
    lo = np.maximum(start[None, :], sel[:, None])
    hi = np.minimum(start[None, :] + CMP_BLOCK, sel[:, None] + SLC_BLOCK)
    ov = np.clip(hi - lo, 0, None) / CMP_BLOCK
    ov[:, n_cmp:] = 0.0
    return jnp.asarray(ov, jnp.bfloat16)


def _forward(x, params, depth):
    B, S, D = x.shape
    T = B * S
    tabs = _tables(S)
    ovt = _overlap_t(S)
    cbias = _cmp_bias(S)
    x2 = x.reshape(T, D)
    final_norm = params['final_norm'][None, :]
    for l in range(depth):
        lw = _prep_layer(params, l)
        (cmp_in, ks, kw, ga, gb, km, qt, vst, vwt, gt, qmt, vmt) = _in_proj(x2, lw, tabs, B, S)
        cmp_t, cmp_r = _compress(cmp_in, lw, B, S)
        o_cmp, selb = _cmp_select(qt, cmp_r, cmp_t, gt, ovt, cbias, B, S)
        o_slc = _flash(qt, ks, vst, B, S, hpk=NSA_REP, kv_per_step=1, selb=selb,
                       gate=gt, gate_blk=1, name="flash_slc")
        o_win = _flash(qt, kw, vwt, B, S, hpk=NSA_REP, kv_per_step=1, window=WINDOW,
                       gate=gt, gate_blk=2, name="flash_win")
        o_b = _flash(qmt, km, vmt, B, S, hpk=1, kv_per_step=4, name="flash_mla")
        x2 = _merge_mlp(o_cmp, o_slc, o_win, o_b, ga, gb, x2, lw, final_norm, final=(l == depth - 1))
    return x2.reshape(B, S, D)


def kernel(x, attn_norm, w_in, cmp_pe, cmp_w1, cmp_w2, nsa_w_o, mla_q_norm, mla_kv_norm, mla_w_uq,
           mla_w_ukv, mla_w_o, w_out, mlp_norm, w_up, w_down, final_norm):
    params = dict(attn_norm=attn_norm, w_in=w_in, cmp_pe=cmp_pe, cmp_w1=cmp_w1, cmp_w2=cmp_w2,
                  nsa_w_o=nsa_w_o, mla_q_norm=mla_q_norm, mla_kv_norm=mla_kv_norm, mla_w_uq=mla_w_uq,
                  mla_w_ukv=mla_w_ukv, mla_w_o=mla_w_o, w_out=w_out, mlp_norm=mlp_norm, w_up=w_up,
                  w_down=w_down, final_norm=final_norm)
    return _forward(x, params, w_in.shape[0])
```

```python
import functools

import numpy as np
import jax
import jax.numpy as jnp
from jax import lax
from jax.experimental import pallas as pl
from jax.experimental.pallas import tpu as pltpu

EPS = 1e-6
ROPE_THETA = 10000.0
NEG = -1e30
TINY = 1e-30
LOG2E = 1.4426950408889634

D_MODEL = 1024
NSA_HEADS = 8
NSA_KV_GROUPS = 2
NSA_REP = NSA_HEADS // NSA_KV_GROUPS
NSA_HEAD_DIM = 64
CMP_BLOCK = 32
CMP_STRIDE = 16
CMP_HIDDEN = 4 * NSA_HEAD_DIM
SLC_BLOCK = 64
SLC_TOPK = 16
N_LOCAL_FORCED = 2
FORCE_BONUS = 1e4
WINDOW = 512

MLA_HEADS = 8
MLA_Q_LORA = 256
MLA_KV_LORA = 128
MLA_NOPE = 64
MLA_ROPE = 32
MLA_V = 64
D_FF = 4 * D_MODEL

LANES = 128
SUBLANES = 8
VMEM_LIMIT_BYTES = 56 * 1024 * 1024

TOKEN_TILE = 512
FLASH_TILE = 512
CMP_TILE = 512
BIAS_ROWS = LANES - NSA_HEAD_DIM
GATE_ROWS = 32

_OFF_Q = 0
_OFF_KVC = _OFF_Q + NSA_HEADS * NSA_HEAD_DIM
_OFF_KVS = _OFF_KVC + 2 * NSA_KV_GROUPS * NSA_HEAD_DIM
_OFF_KVW = _OFF_KVS + 2 * NSA_KV_GROUPS * NSA_HEAD_DIM
_OFF_NG = _OFF_KVW + 2 * NSA_KV_GROUPS * NSA_HEAD_DIM
_OFF_CQ = _OFF_NG + 3 * NSA_HEADS
_OFF_CKV = _OFF_CQ + MLA_Q_LORA
_OFF_KR = _OFF_CKV + MLA_KV_LORA
_OFF_GA = _OFF_KR + MLA_ROPE
_OFF_GB = _OFF_GA + D_MODEL

_R_CMP = 0
_R_KS = _R_CMP + 2 * NSA_KV_GROUPS * NSA_HEAD_DIM
_R_KW = _R_KS + NSA_KV_GROUPS * LANES
_R_CQ = _R_KW + NSA_KV_GROUPS * LANES
_R_CKV = _R_CQ + MLA_Q_LORA
_R_KPE = _R_CKV + MLA_KV_LORA
_R_GA = _R_KPE + LANES
_R_GB = _R_GA + D_MODEL
_R_END = _R_GB + D_MODEL

_T_Q = 0
_T_VS = _T_Q + NSA_HEADS * NSA_HEAD_DIM
_T_VW = _T_VS + NSA_KV_GROUPS * NSA_HEAD_DIM
_T_NG = _T_VW + NSA_KV_GROUPS * NSA_HEAD_DIM
_T_END = _T_NG + GATE_ROWS


def _nt_dot(a, b):
    return lax.dot_general(a, b, (((1,), (1,)), ((), ())), preferred_element_type=jnp.float32)


def _dot(a, b):
    return jnp.dot(a, b, preferred_element_type=jnp.float32)


def _rms(xf, g):
    return xf * lax.rsqrt(jnp.mean(xf * xf, axis=-1, keepdims=True) + EPS) * g


def _rope_rows(y, c, s1, s2, sh):
    return y * c + pltpu.roll(y, sh, 1) * s1 + pltpu.roll(y, LANES - sh, 1) * s2


def _rope_cols(y, cos_t, sin_t):
    h = y.shape[0] // 2
    a, b = y[:h], y[h:]
    return jnp.concatenate([a * cos_t - b * sin_t, a * sin_t + b * cos_t], axis=0)


def _in_proj_kernel(x_ref, g_ref, w_ref, wt_ref, qng_ref, kvng_ref, wuqt_ref, wvt_ref, wk_ref,
                    cc_ref, s1c_ref, s2c_ref, ck_ref, s1k_ref, s2k_ref, ckp_ref, s1kp_ref, s2kp_ref,
                    cosq_ref, sinq_ref, cosm_ref, sinm_ref,
                    cmp_ref, ks_ref, kw_ref, ga_ref, gb_ref, km_ref,
                    qt_ref, vst_ref, vwt_ref, gt_ref, qmt_ref, vmt_ref, *, tm, tk, s_tiles):
    si = pl.program_id(0) % s_tiles
    hd = NSA_HEAD_DIM
    bf = jnp.bfloat16
    xn = _rms(x_ref[...], g_ref[...]).astype(bf)

    y = _dot(xn, w_ref[:, _R_CMP:_R_KS])
    cmp_ref[:, :LANES] = _rope_rows(y[:, :LANES], cc_ref[...], s1c_ref[...], s2c_ref[...], hd // 2)
    cmp_ref[:, LANES:] = y[:, LANES:]

    lane = lax.broadcasted_iota(jnp.int32, (tm, LANES), 1)
    pos = si * tm + lax.broadcasted_iota(jnp.int32, (tm, LANES), 0)
    sel_cols = jnp.where(lane - hd == pos // SLC_BLOCK, 1.0, 0.0)
    ck, s1k, s2k = ck_ref[...], s1k_ref[...], s2k_ref[...]
    y = _dot(xn, w_ref[:, _R_KS:_R_KW])
    for g in range(NSA_KV_GROUPS):
        blk = _rope_rows(y[:, g * LANES:(g + 1) * LANES], ck, s1k, s2k, hd // 2)
        ks_ref[:, g * LANES:(g + 1) * LANES] = (blk + sel_cols).astype(bf)
    y = _dot(xn, w_ref[:, _R_KW:_R_CQ])
    for g in range(NSA_KV_GROUPS):
        kw_ref[:, g * LANES:(g + 1) * LANES] = _rope_rows(y[:, g * LANES:(g + 1) * LANES], ck, s1k, s2k,
                                                          hd // 2).astype(bf)

    ga_ref[...] = jax.nn.sigmoid(_dot(xn, w_ref[:, _R_GA:_R_GB])).astype(bf)
    gb_ref[...] = jax.nn.sigmoid(_dot(xn, w_ref[:, _R_GB:_R_END])).astype(bf)

    cqn = _rms(_dot(xn, w_ref[:, _R_CQ:_R_CKV]), qng_ref[...]).astype(bf)
    ckvn = _rms(_dot(xn, w_ref[:, _R_CKV:_R_KPE]), kvng_ref[...]).astype(bf)
    kpe = _rope_rows(_dot(xn, w_ref[:, _R_KPE:_R_GA]), ckp_ref[...], s1kp_ref[...], s2kp_ref[...], MLA_ROPE // 2)
    y = _dot(ckvn, wk_ref[...])
    for h in range(MLA_HEADS):
        km_ref[:, h * LANES:(h + 1) * LANES] = (y[:, h * LANES:(h + 1) * LANES] + kpe).astype(bf)

    yt = _nt_dot(wt_ref[...], xn)
    cosq, sinq = cosq_ref[...], sinq_ref[...]
    zero_rows = jnp.zeros((LANES - hd, tm), jnp.float32)
    for h in range(NSA_HEADS):
        q = _rope_cols(yt[_T_Q + h * hd:_T_Q + (h + 1) * hd], cosq, sinq)
        qt_ref[0, h] = jnp.concatenate([q, zero_rows], axis=0).astype(bf)
    ones_rows = jnp.where(lax.broadcasted_iota(jnp.int32, (LANES - hd, tm), 0) == 0, 1.0, 0.0)
    for g in range(NSA_KV_GROUPS):
        vs = jnp.concatenate([yt[_T_VS + g * hd:_T_VS + (g + 1) * hd], ones_rows], axis=0).astype(bf)
        vw = jnp.concatenate([yt[_T_VW + g * hd:_T_VW + (g + 1) * hd], ones_rows], axis=0).astype(bf)
        for c in range(tm // tk):
            vst_ref[0, g, c] = vs[:, c * tk:(c + 1) * tk]
            vwt_ref[0, g, c] = vw[:, c * tk:(c + 1) * tk]
    gt_ref[...] = jax.nn.sigmoid(yt[_T_NG:_T_END])

    dq = MLA_NOPE + MLA_ROPE
    sm = dq ** -0.5 * LOG2E
    ymt = _nt_dot(wuqt_ref[...], cqn)
    cosm, sinm = cosm_ref[...], sinm_ref[...]
    pad_rows = jnp.zeros((LANES - dq, tm), jnp.float32)
    for h in range(MLA_HEADS):
        nope = ymt[h * dq:h * dq + MLA_NOPE] * sm
        pe = _rope_cols(ymt[h * dq + MLA_NOPE:(h + 1) * dq], cosm, sinm)
        qmt_ref[0, h] = jnp.concatenate([nope, pe, pad_rows], axis=0).astype(bf)
    yvt = _nt_dot(wvt_ref[...], ckvn)
    for h in range(MLA_HEADS):
        vm = jnp.concatenate([yvt[h * MLA_V:(h + 1) * MLA_V], ones_rows], axis=0).astype(bf)
        for c in range(tm // tk):
            vmt_ref[0, h, c] = vm[:, c * tk:(c + 1) * tk]


def _in_proj(x2, lw, tabs, B, S):
    T = B * S
    tm, tk = TOKEN_TILE, FLASH_TILE
    s_tiles = S // tm
    nkt = S // tk
    bf = jnp.bfloat16

    def full(a):
        return pl.BlockSpec(a.shape, lambda i, _n=a.ndim: (0,) * _n)

    def rows(width):
        return pl.BlockSpec((tm, width), lambda i: (i, 0))

    def tab_rows(a):
        return pl.BlockSpec((tm, a.shape[1]), lambda i: (i % s_tiles, 0))

    def tab_cols(a):
        return pl.BlockSpec((a.shape[0], tm), lambda i: (0, i % s_tiles))

    def qt_spec(n):
        return pl.BlockSpec((1, n, LANES, tm), lambda i: (i // s_tiles, 0, 0, i % s_tiles))

    def vt_spec(n):
        return pl.BlockSpec((1, n, tm // tk, LANES, tk), lambda i: (i // s_tiles, 0, i % s_tiles, 0, 0))

    weights = [lw['attn_norm'], lw['w_row'], lw['w_t'], lw['q_norm'], lw['kv_norm'],
               lw['w_uqt'], lw['w_vt'], lw['w_k']]
    row_tabs = [tabs[k] for k in ('cc', 's1c', 's2c', 'ck', 's1k', 's2k', 'ckp', 's1kp', 's2kp')]
    col_tabs = [tabs[k] for k in ('cosq_t', 'sinq_t', 'cosm_t', 'sinm_t')]
    in_specs = ([rows(D_MODEL)] + [full(w) for w in weights] + [tab_rows(t) for t in row_tabs]
                + [tab_cols(t) for t in col_tabs])
    G, H = NSA_KV_GROUPS, MLA_HEADS
    out_shape = [
        jax.ShapeDtypeStruct((T, 2 * G * NSA_HEAD_DIM), jnp.float32),
        jax.ShapeDtypeStruct((T, G * LANES), bf),
        jax.ShapeDtypeStruct((T, G * LANES), bf),
        jax.ShapeDtypeStruct((T, D_MODEL), bf),
        jax.ShapeDtypeStruct((T, D_MODEL), bf),
        jax.ShapeDtypeStruct((T, H * LANES), bf),
        jax.ShapeDtypeStruct((B, NSA_HEADS, LANES, S), bf),
        jax.ShapeDtypeStruct((B, G, nkt, LANES, tk), bf),
        jax.ShapeDtypeStruct((B, G, nkt, LANES, tk), bf),
        jax.ShapeDtypeStruct((GATE_ROWS, T), jnp.float32),
        jax.ShapeDtypeStruct((B, H, LANES, S), bf),
        jax.ShapeDtypeStruct((B, H, nkt, LANES, tk), bf),
    ]
    out_specs = [rows(2 * G * NSA_HEAD_DIM), rows(G * LANES), rows(G * LANES), rows(D_MODEL), rows(D_MODEL),
                 rows(H * LANES), qt_spec(NSA_HEADS), vt_spec(G), vt_spec(G),
                 pl.BlockSpec((GATE_ROWS, tm), lambda i: (0, i)), qt_spec(H), vt_spec(H)]
    return pl.pallas_call(
        functools.partial(_in_proj_kernel, tm=tm, tk=tk, s_tiles=s_tiles),
        grid=(T // tm,),
        in_specs=in_specs,
        out_specs=out_specs,
        out_shape=out_shape,
        compiler_params=pltpu.CompilerParams(dimension_semantics=("arbitrary",),
                                             vmem_limit_bytes=VMEM_LIMIT_BYTES),
        name="in_proj",
    )(x2, *weights, *row_tabs, *col_tabs)


def _compress_kernel(x_ref, pe_ref, w1_ref, w2_ref, w2t_ref, ot_ref, or_ref, xs_scr, *, seq, nc):
    G = NSA_KV_GROUPS
    bf = jnp.bfloat16
    for kv in range(2):
        xs_scr[kv, pl.ds(0, seq), :] = x_ref[:, kv * LANES:(kv + 1) * LANES]
        xs_scr[kv, pl.ds(seq, CMP_STRIDE), :] = jnp.zeros((CMP_STRIDE, LANES), jnp.float32)
    acc = [jnp.zeros((nc, CMP_HIDDEN), jnp.float32) for _ in range(2 * G)]
    for l in range(CMP_BLOCK):
        for kv in range(2):
            rows = xs_scr[kv, pl.ds(l, nc, stride=CMP_STRIDE), :] + pe_ref[l, :, kv * LANES:(kv + 1) * LANES]
            rows = rows.astype(bf)
            for g in range(G):
                acc[kv * G + g] = acc[kv * G + g] + _dot(rows, w1_ref[kv, g, l])
    for c in range(2 * G):
        h = acc[c]
        h = (h * jax.nn.sigmoid(h)).astype(bf)
        ot_ref[0, c] = _nt_dot(w2t_ref[c // G], h).astype(bf)
        or_ref[0, c] = _dot(h, w2_ref[c // G]).astype(bf)


def _compress(cmp_in, lw, B, S):
    nc = S // CMP_STRIDE
    bf = jnp.bfloat16
    width = cmp_in.shape[1]

    def resident(a):
        return pl.BlockSpec(a.shape, lambda b, _n=a.ndim: (0,) * _n, pipeline_mode=pl.Buffered(1))

    weights = [lw['cmp_pe'], lw['cmp_w1'], lw['cmp_w2'], lw['cmp_w2t']]
    return pl.pallas_call(
        functools.partial(_compress_kernel, seq=S, nc=nc),
        grid=(B,),
        in_specs=[pl.BlockSpec((S, width), lambda b: (b, 0))] + [resident(a) for a in weights],
        out_specs=[pl.BlockSpec((1, 2 * NSA_KV_GROUPS, LANES, nc), lambda b: (b, 0, 0, 0)),
                   pl.BlockSpec((1, 2 * NSA_KV_GROUPS, nc, LANES), lambda b: (b, 0, 0, 0))],
        out_shape=[jax.ShapeDtypeStruct((B, 2 * NSA_KV_GROUPS, LANES, nc), bf),
                   jax.ShapeDtypeStruct((B, 2 * NSA_KV_GROUPS, nc, LANES), bf)],
        scratch_shapes=[pltpu.VMEM((2, S + CMP_STRIDE, LANES), jnp.float32)],
        compiler_params=pltpu.CompilerParams(dimension_semantics=("arbitrary",),
                                             vmem_limit_bytes=VMEM_LIMIT_BYTES),
        name="compress",
    )(cmp_in, *weights)


def _cmp_select_kernel(qt_ref, kc_ref, vct_ref, gate_ref, ovt_ref, cbias_ref, o_ref, selb_ref,
                       score_scr, cnt_scr, *, tq, nc, ns):
    g = pl.program_id(1)
    qi = pl.program_id(2)
    hd = NSA_HEAD_DIM
    kc = kc_ref[0, 0]
    vct = vct_ref[0, 0]
    cbias = cbias_ref[...]
    qpos = qi * tq + lax.broadcasted_iota(jnp.int32, (1, tq), 1)
    any_valid = jnp.where(qpos >= CMP_BLOCK - 1, 1.0, 0.0)

    psum = jnp.zeros((nc, tq), jnp.float32)
    outs = []
    for r in range(NSA_REP):
        s = _dot(kc, qt_ref[0, r]) + cbias
        e = jnp.exp2(s - jnp.max(s, axis=0, keepdims=True))
        p = e * (any_valid / jnp.maximum(jnp.sum(e, axis=0, keepdims=True), TINY))
        psum = psum + p
        o = _dot(vct, p.astype(jnp.bfloat16))
        gate = jnp.where(g == 0, gate_ref[r:r + 1, :], gate_ref[NSA_REP + r:NSA_REP + r + 1, :])
        outs.append(o[:hd] * gate)
    o_ref[...] = jnp.concatenate(outs, axis=0).T.astype(jnp.bfloat16)

    hi = psum.astype(jnp.bfloat16)
    lo = (psum - hi.astype(jnp.float32)).astype(jnp.bfloat16)
    ovt = ovt_ref[...]
    imp = _dot(ovt, hi) + _dot(ovt, lo)
    post = qi * tq + lax.broadcasted_iota(jnp.int32, (ns, tq), 1)
    blk = lax.broadcasted_iota(jnp.int32, (ns, tq), 0)
    cur = post // SLC_BLOCK
    causal = blk <= cur
    forced = (blk == 0) | ((cur - blk >= 0) & (cur - blk < N_LOCAL_FORCED))
    score_scr[...] = jnp.where(causal, imp + jnp.where(forced, FORCE_BONUS, 0.0), NEG)

    n_act = (tq // SLC_BLOCK * (qi + 1) + SUBLANES - 1) // SUBLANES
    sub_id = lax.broadcasted_iota(jnp.int32, (SUBLANES, tq), 0)
    for rg in range(ns // SUBLANES):
        rows = pl.ds(rg * SUBLANES, SUBLANES)

        @pl.when(rg < n_act)
        def _():
            sub = score_scr[rows, :]
            c = jnp.zeros((SUBLANES, tq), jnp.float32)
            for i in range((rg + 1) * SUBLANES):
                bi = score_scr[i:i + 1, :]
                if i < rg * SUBLANES:
                    ahead = bi >= sub
                else:
                    ahead = (bi > sub) | ((bi == sub) & (sub_id > (i - rg * SUBLANES)))
                c = c + jnp.where(ahead, 1.0, 0.0)

            def later_group(ig, c):
                for ii in range(SUBLANES):
                    bi = score_scr[pl.ds(ig * SUBLANES + ii, 1), :]
                    c = c + jnp.where(bi > sub, 1.0, 0.0)
                return c

            cnt_scr[rows, :] = lax.fori_loop(rg + 1, n_act, later_group, c)

        @pl.when(rg >= n_act)
        def _():
            cnt_scr[rows, :] = jnp.full((SUBLANES, tq), float(ns), jnp.float32)

    sel = (cnt_scr[...] < float(min(SLC_TOPK, ns))) & causal
    parts = [jnp.where(sel, 0.0, NEG)]
    if BIAS_ROWS - ns > 0:
        parts.append(jnp.zeros((BIAS_ROWS - ns, tq), jnp.float32))
    selb_ref[0, 0] = jnp.concatenate(parts, axis=0).astype(jnp.bfloat16)


def _cmp_bias(S):
    n = np.arange(S // CMP_STRIDE)[:, None]
    pos = np.arange(S)[None, :]
    return jnp.asarray(np.where(n * CMP_STRIDE + (CMP_BLOCK - 1) <= pos, 0.0, NEG), jnp.float32)


def _cmp_select(qt, kc, vct, gt, ovt, cbias, B, S):
    T = B * S
    tq = CMP_TILE
    nq = S // tq
    nc = kc.shape[2]
    ns = S // SLC_BLOCK
    assert ns <= BIAS_ROWS and ns % SUBLANES == 0
    G = NSA_KV_GROUPS
    bf = jnp.bfloat16
    return pl.pallas_call(
        functools.partial(_cmp_select_kernel, tq=tq, nc=nc, ns=ns),
        grid=(B, G, nq),
        in_specs=[
            pl.BlockSpec((1, NSA_REP, LANES, tq), lambda b, g, i: (b, g, 0, i)),
            pl.BlockSpec((1, 1, nc, LANES), lambda b, g, i: (b, g, 0, 0)),
            pl.BlockSpec((1, 1, LANES, nc), lambda b, g, i: (b, G + g, 0, 0)),
            pl.BlockSpec((NSA_HEADS, tq), lambda b, g, i: (0, b * nq + i)),
            pl.BlockSpec((ns, nc), lambda b, g, i: (0, 0)),
            pl.BlockSpec((nc, tq), lambda b, g, i: (0, i)),
        ],
        out_specs=[pl.BlockSpec((tq, NSA_REP * NSA_HEAD_DIM), lambda b, g, i: (b * nq + i, g)),
                   pl.BlockSpec((1, 1, BIAS_ROWS, tq), lambda b, g, i: (b, g, 0, i))],
        out_shape=[jax.ShapeDtypeStruct((T, NSA_HEADS * NSA_HEAD_DIM), bf),
                   jax.ShapeDtypeStruct((B, G, BIAS_ROWS, S), bf)],
        scratch_shapes=[pltpu.VMEM((ns, tq), jnp.float32), pltpu.VMEM((ns, tq), jnp.float32)],
        compiler_params=pltpu.CompilerParams(dimension_semantics=("arbitrary",) * 3,
                                             vmem_limit_bytes=VMEM_LIMIT_BYTES),
        name="cmp_select",
    )(qt, kc, vct, gt, ovt, cbias)


def _flash_kernel(*refs, hpk, kv_per_step, tq, tk, window, has_sel, has_gate):
    refs = list(refs)
    qt_ref = refs.pop(0)
    selb_ref = refs.pop(0) if has_sel else None
    k_ref = refs.pop(0)
    vt_ref = refs.pop(0)
    bias_ref = refs.pop(0)
    gate_ref = refs.pop(0) if has_gate else None
    o_ref, q_scr, p_scr, m_scr, alpha_scr, acc_scr = refs
    nh = hpk * kv_per_step
    hd = NSA_HEAD_DIM
    j = pl.program_id(1)
    qi = pl.program_id(2)

    for h in range(nh):
        q = qt_ref[0, h]
        if has_sel:
            q = jnp.concatenate([q[:hd], selb_ref[0, 0]], axis=0)
        q_scr[h] = q
    m_scr[...] = jnp.full(m_scr.shape, NEG, jnp.float32)
    acc_scr[...] = jnp.zeros(acc_scr.shape, jnp.float32)

    def values(kt, h, slot):
        vtile = vt_ref[0, h // hpk, kt]
        acc_scr[h] = alpha_scr[slot, h] * acc_scr[h] + _dot(vtile, p_scr[slot, h])

    def tile_step(kt, bias_idx, prev, slot):
        if bias_idx is not None:
            masked_tile_step(kt, bias_idx, prev, slot)
            return
        s_all = []
        for p in range(kv_per_step):
            ktile = k_ref[pl.ds(pl.multiple_of(kt * tk, tk), tk), p * LANES:(p + 1) * LANES]
            for r in range(hpk):
                s_all.append(_dot(ktile, q_scr[p * hpk + r]))
        for h in range(nh):
            if prev is not None:
                values(prev, h, 1 - slot)
            s = s_all[h]
            m_prev = m_scr[h]
            m_new = jnp.maximum(m_prev, jnp.max(s, axis=0, keepdims=True))
            alpha_scr[slot, h] = jnp.exp2(m_prev - m_new)
            p_scr[slot, h] = jnp.exp2(s - m_new).astype(jnp.bfloat16)
            m_scr[h] = m_new

    def masked_tile_step(kt, bias_idx, prev, slot):
        hk, hq = tk // 2, tq // 2
        keys = (slice(0, hk), slice(hk, tk))
        if bias_idx == 0:
            cols = (slice(0, tq), slice(hq, tq))
            dead = (slice(hk, tk), slice(0, hq))
        else:
            cols = (slice(0, hq), slice(0, tq))
            dead = (slice(0, hk), slice(hq, tq))
        s_all = []
        for p in range(kv_per_step):
            ktiles = [k_ref[pl.ds(pl.multiple_of(kt * tk + kk.start, hk), hk), p * LANES:(p + 1) * LANES]
                      for kk in keys]
            for r in range(hpk):
                q = q_scr[p * hpk + r]
                s_all.append([_dot(ktiles[i], q[:, cols[i]]) + bias_ref[bias_idx, keys[i], cols[i]]
                              for i in range(2)])
        for h in range(nh):
            if prev is not None:
                values(prev, h, 1 - slot)
            full, half = (0, 1) if bias_idx == 0 else (1, 0)
            s_full, s_half = s_all[h][full], s_all[h][half]
            top = jnp.max(s_full, axis=0, keepdims=True)
            part = jnp.max(s_half, axis=0, keepdims=True)
            neg = jnp.full((1, hq), NEG, jnp.float32)
            part = jnp.concatenate([neg, part] if bias_idx == 0 else [part, neg], axis=1)
            m_prev = m_scr[h]
            m_new = jnp.maximum(m_prev, jnp.maximum(top, part))
            alpha_scr[slot, h] = jnp.exp2(m_prev - m_new)
            p_scr[slot, h, keys[full], :] = jnp.exp2(s_full - m_new).astype(jnp.bfloat16)
            p_scr[slot, h, keys[half], cols[half]] = jnp.exp2(s_half - m_new[:, cols[half]]).astype(jnp.bfloat16)
            p_scr[slot, h, dead[0], dead[1]] = jnp.zeros((hk, hq), jnp.bfloat16)
            m_scr[h] = m_new

    tile_step(qi, 0, None, 0)
    if window is None:
        lo, n_extra, bias_idx = 0, qi, None
    else:
        lo, n_extra, bias_idx = qi - 1, jnp.minimum(qi, 1), 1

    def pair(i, carry):
        kt = lo + 2 * i
        tile_step(kt, bias_idx, jnp.where(i == 0, qi, kt - 1), 1)
        tile_step(kt + 1, bias_idx, kt, 0)
        return carry

    lax.fori_loop(0, n_extra // 2, pair, 0)
    last = lo + n_extra - 1

    @pl.when(n_extra % 2 == 1)
    def _():
        tile_step(last, bias_idx, jnp.where(n_extra == 1, qi, last - 1), 1)
        for h in range(nh):
            values(last, h, 1)

    @pl.when(n_extra % 2 == 0)
    def _():
        for h in range(nh):
            values(jnp.where(n_extra == 0, qi, last), h, 0)

    outs = []
    for h in range(nh):
        acc = acc_scr[h]
        o = acc[:hd] / jnp.maximum(acc[hd:hd + 1], TINY)
        if has_gate:
            o = o * jnp.where(j == 0, gate_ref[h:h + 1, :], gate_ref[nh + h:nh + h + 1, :])
        outs.append(o)
    o_ref[...] = jnp.concatenate(outs, axis=0).T.astype(jnp.bfloat16)


def _tile_biases(t):
    k = np.arange(t)[:, None]
    q = np.arange(t)[None, :]
    return jnp.asarray(np.stack([np.where(k <= q, 0.0, NEG), np.where(k > q, 0.0, NEG)]), jnp.float32)


def _flash(qt, k, vt, B, S, *, hpk, kv_per_step, window=None, selb=None, gate=None, gate_blk=0, name):
    T = B * S
    tq = tk = FLASH_TILE
    assert window is None or window == tk
    nq = S // tq
    nkv = vt.shape[1]
    nh = hpk * kv_per_step
    steps = nkv // kv_per_step
    hd = NSA_HEAD_DIM
    in_specs = [pl.BlockSpec((1, nh, LANES, tq), lambda b, j, i: (b, j, 0, i))]
    args = [qt]
    if selb is not None:
        in_specs.append(pl.BlockSpec((1, 1, BIAS_ROWS, tq), lambda b, j, i: (b, j, 0, i)))
        args.append(selb)
    in_specs.append(pl.BlockSpec((S, kv_per_step * LANES), lambda b, j, i: (b, j)))
    in_specs.append(pl.BlockSpec((1, kv_per_step, S // tk, LANES, tk), lambda b, j, i: (b, j, 0, 0, 0)))
    in_specs.append(pl.BlockSpec((2, tk, tq), lambda b, j, i: (0, 0, 0)))
    args += [k, vt, _tile_biases(tq)]
    if gate is not None:
        in_specs.append(pl.BlockSpec((NSA_HEADS, tq), lambda b, j, i: (gate_blk, b * nq + i)))
        args.append(gate)
    return pl.pallas_call(
        functools.partial(_flash_kernel, hpk=hpk, kv_per_step=kv_per_step, tq=tq, tk=tk, window=window,
                          has_sel=selb is not None, has_gate=gate is not None),
        grid=(B, steps, nq),
        in_specs=in_specs,
        out_specs=pl.BlockSpec((tq, nh * hd), lambda b, j, i: (b * nq + i, j)),
        out_shape=jax.ShapeDtypeStruct((T, nkv * hpk * hd), jnp.bfloat16),
        scratch_shapes=[pltpu.VMEM((nh, LANES, tq), jnp.bfloat16),
                        pltpu.VMEM((2, nh, tk, tq), jnp.bfloat16),
                        pltpu.VMEM((nh, 1, tq), jnp.float32),
                        pltpu.VMEM((2, nh, 1, tq), jnp.float32),
                        pltpu.VMEM((nh, LANES, tq), jnp.float32)],
        compiler_params=pltpu.CompilerParams(dimension_semantics=("arbitrary",) * 3,
                                             vmem_limit_bytes=VMEM_LIMIT_BYTES),
        name=name,
    )(*args)


def _merge_mlp_kernel(oc_ref, os_ref, ow_ref, ob_ref, ga_ref, gb_ref, x_ref, wa_ref, wb_ref, wo_ref,
                      g_ref, wu_ref, wd_ref, gf_ref, out_ref, *, ff_chunk, final):
    f32 = jnp.float32
    bf = jnp.bfloat16
    o_a = (oc_ref[...].astype(f32) + os_ref[...].astype(f32) + ow_ref[...].astype(f32)).astype(bf)
    merged = (ga_ref[...].astype(f32) * _dot(o_a, wa_ref[...])
              + gb_ref[...].astype(f32) * _dot(ob_ref[...], wb_ref[...]))
    x = x_ref[...] + _dot(merged.astype(bf), wo_ref[...])
    hn = _rms(x, g_ref[...]).astype(bf)
    acc = jnp.zeros(x.shape, f32)
    for c in range(D_FF // ff_chunk):
        h = jnp.maximum(_dot(hn, wu_ref[:, c * ff_chunk:(c + 1) * ff_chunk]), 0.0)
        acc = acc + _dot((h * h).astype(bf), wd_ref[c * ff_chunk:(c + 1) * ff_chunk, :])
    y = x + acc
    if final:
        y = _rms(y, gf_ref[...])
    out_ref[...] = y


def _merge_mlp(o_cmp, o_slc, o_win, o_b, ga, gb, x2, lw, final_norm, final):
    T = x2.shape[0]
    tm = TOKEN_TILE

    def rows(width):
        return pl.BlockSpec((tm, width), lambda i: (i, 0))

    def resident(a):
        return pl.BlockSpec(a.shape, lambda i: (0, 0), pipeline_mode=pl.Buffered(1))

    w = NSA_HEADS * NSA_HEAD_DIM
    weights = [lw['nsa_w_o'], lw['mla_w_o'], lw['w_out'], lw['mlp_norm'], lw['w_up'], lw['w_down'], final_norm]
    return pl.pallas_call(
        functools.partial(_merge_mlp_kernel, ff_chunk=1024, final=final),
        grid=(T // tm,),
        in_specs=[rows(w), rows(w), rows(w), rows(MLA_HEADS * MLA_V), rows(D_MODEL), rows(D_MODEL),
                  rows(D_MODEL)] + [resident(a) for a in weights],
        out_specs=rows(D_MODEL),
        out_shape=jax.ShapeDtypeStruct((T, D_MODEL), jnp.float32),
        compiler_params=pltpu.CompilerParams(dimension_semantics=("arbitrary",),
                                             vmem_limit_bytes=VMEM_LIMIT_BYTES),
        name="merge_mlp",
    )(o_cmp, o_slc, o_win, o_b, ga, gb, x2, *weights)


def _pad_cols(w, width, left=0):
    return jnp.pad(w, ((0, 0), (left, width - left - w.shape[1])))


def _prep_layer(p, l):
    bf = jnp.bfloat16
    hd = NSA_HEAD_DIM
    G = NSA_KV_GROUPS
    w = p['w_in'][l]
    cols = [w[:, _OFF_KVC:_OFF_KVS]]
    for off in (_OFF_KVS, _OFF_KVW):
        for g in range(G):
            cols.append(_pad_cols(w[:, off + g * hd:off + (g + 1) * hd], LANES))
    cols.append(w[:, _OFF_CQ:_OFF_CKV])
    cols.append(w[:, _OFF_CKV:_OFF_KR])
    cols.append(_pad_cols(w[:, _OFF_KR:_OFF_GA], LANES, left=MLA_NOPE))
    cols.append(w[:, _OFF_GA:_OFF_GB])
    cols.append(w[:, _OFF_GB:_OFF_GB + D_MODEL])
    w_row = jnp.concatenate(cols, axis=1).astype(bf)

    t_cols = [w[:, _OFF_Q:_OFF_KVC]]
    t_cols += [w[:, off + (G + g) * hd:off + (G + g + 1) * hd] for off in (_OFF_KVS, _OFF_KVW) for g in range(G)]
    t_cols.append(_pad_cols(w[:, _OFF_NG:_OFF_CQ], GATE_ROWS))
    w_t = jnp.concatenate(t_cols, axis=1).T.astype(bf)

    wukv = p['mla_w_ukv'][l]
    dkv = MLA_NOPE + MLA_V
    w_k = jnp.concatenate([_pad_cols(wukv[:, h * dkv:h * dkv + MLA_NOPE], LANES) for h in range(MLA_HEADS)], axis=1)
    w_vt = jnp.concatenate([wukv[:, h * dkv + MLA_NOPE:(h + 1) * dkv] for h in range(MLA_HEADS)], axis=1).T

    assert G * hd == LANES
    pe = p['cmp_pe'][l]
    cmp_pe = jnp.concatenate([pe[0]] * G + [pe[1]] * G, axis=1)[:, None, :]
    w1 = p['cmp_w1'][l].reshape(2, CMP_BLOCK, hd, CMP_HIDDEN)
    cmp_w1 = jnp.stack([jnp.pad(w1, ((0, 0), (0, 0), (g * hd, LANES - (g + 1) * hd), (0, 0)))
                        for g in range(G)], axis=1).astype(bf)
    cmp_w2 = p['cmp_w2'][l]
    cmp_w2p = jnp.pad(cmp_w2, ((0, 0), (0, 0), (0, LANES - hd)))
    return {
        'attn_norm': p['attn_norm'][l][None, :],
        'w_row': w_row, 'w_t': w_t,
        'q_norm': p['mla_q_norm'][l][None, :], 'kv_norm': p['mla_kv_norm'][l][None, :],
        'w_uqt': p['mla_w_uq'][l].T.astype(bf), 'w_vt': w_vt.astype(bf), 'w_k': w_k.astype(bf),
        'cmp_pe': cmp_pe, 'cmp_w1': cmp_w1,
        'cmp_w2': cmp_w2p.astype(bf),
        'cmp_w2t': jnp.swapaxes(cmp_w2p, 1, 2).astype(bf),
        'nsa_w_o': p['nsa_w_o'][l].astype(bf), 'mla_w_o': p['mla_w_o'][l].astype(bf),
        'w_out': p['w_out'][l].astype(bf),
        'mlp_norm': p['mlp_norm'][l][None, :],
        'w_up': p['w_up'][l].astype(bf), 'w_down': p['w_down'][l].astype(bf),
    }


def _tables(S):
    pos = jnp.arange(S, dtype=jnp.int32).astype(jnp.float32)
    hd = NSA_HEAD_DIM

    def angles(d):
        inv = ROPE_THETA ** (-jnp.arange(0, d, 2, dtype=jnp.float32) / d)
        ang = pos[:, None] * inv[None, :]
        return jnp.cos(ang), jnp.sin(ang)

    cos, sin = angles(hd)
    z = jnp.zeros_like(cos)
    pad = jnp.zeros((S, LANES - hd), jnp.float32)
    cc = jnp.concatenate([cos, cos, cos, cos], axis=1)
    s1c = jnp.concatenate([z, sin, z, sin], axis=1)
    s2c = jnp.concatenate([-sin, z, -sin, z], axis=1)
    ck = jnp.concatenate([cos, cos, pad], axis=1)
    s1k = jnp.concatenate([z, sin, pad], axis=1)
    s2k = jnp.concatenate([-sin, z, pad], axis=1)
    q_scale = hd ** -0.5 * LOG2E

    cos16, sin16 = angles(MLA_ROPE)
    z16 = jnp.zeros_like(cos16)
    z64 = jnp.zeros((S, MLA_NOPE), jnp.float32)
    pad32 = jnp.zeros((S, LANES - MLA_NOPE - MLA_ROPE), jnp.float32)
    ckp = jnp.concatenate([z64, cos16, cos16, pad32], axis=1)
    s1kp = jnp.concatenate([z64, z16, sin16, pad32], axis=1)
    s2kp = jnp.concatenate([z64, -sin16, z16, pad32], axis=1)
    m_scale = (MLA_NOPE + MLA_ROPE) ** -0.5 * LOG2E
    return {'cc': cc, 's1c': s1c, 's2c': s2c, 'ck': ck, 's1k': s1k, 's2k': s2k,
            'ckp': ckp, 's1kp': s1kp, 's2kp': s2kp,
            'cosq_t': cos.T * q_scale, 'sinq_t': sin.T * q_scale,
            'cosm_t': cos16.T * m_scale, 'sinm_t': sin16.T * m_scale}


def _overlap_t(S):
    nc = S // CMP_STRIDE
    n_cmp = (S - CMP_BLOCK) // CMP_STRIDE + 1
    ns = S // SLC_BLOCK
    start = np.arange(nc) * CMP_STRIDE
    sel = np.arange(ns) * SLC_BLOCK
    lo = np.maximum(start[None, :], sel[:, None])
    hi = np.minimum(start[None, :] + CMP_BLOCK, sel[:, None] + SLC_BLOCK)
    ov = np.clip(hi - lo, 0, None) / CMP_BLOCK
    ov[:, n_cmp:] = 0.0
    return jnp.asarray(ov, jnp.bfloat16)


def _forward(x, params, depth):
    B, S, D = x.shape
    T = B * S
    tabs = _tables(S)
    ovt = _overlap_t(S)
    cbias = _cmp_bias(S)
    x2 = x.reshape(T, D)
    final_norm = params['final_norm'][None, :]
    for l in range(depth):
        lw = _prep_layer(params, l)
        (cmp_in, ks, kw, ga, gb, km, qt, vst, vwt, gt, qmt, vmt) = _in_proj(x2, lw, tabs, B, S)
        cmp_t, cmp_r = _compress(cmp_in, lw, B, S)
        o_cmp, selb = _cmp_select(qt, cmp_r, cmp_t, gt, ovt, cbias, B, S)
        o_slc = _flash(qt, ks, vst, B, S, hpk=NSA_REP, kv_per_step=1, selb=selb,
                       gate=gt, gate_blk=1, name="flash_slc")
        o_win = _flash(qt, kw, vwt, B, S, hpk=NSA_REP, kv_per_step=1, window=WINDOW,
                       gate=gt, gate_blk=2, name="flash_win")
        o_b = _flash(qmt, km, vmt, B, S, hpk=1, kv_per_step=4, name="flash_mla")
        x2 = _merge_mlp(o_cmp, o_slc, o_win, o_b, ga, gb, x2, lw, final_norm, final=(l == depth - 1))
    return x2.reshape(B, S, D)


def kernel(x, attn_norm, w_in, cmp_pe, cmp_w1, cmp_w2, nsa_w_o, mla_q_norm, mla_kv_norm, mla_w_uq,
           mla_w_ukv, mla_w_o, w_out, mlp_norm, w_up, w_down, final_norm):
    params = dict(attn_norm=attn_norm, w_in=w_in, cmp_pe=cmp_pe, cmp_w1=cmp_w1, cmp_w2=cmp_w2,
                  nsa_w_o=nsa_w_o, mla_q_norm=mla_q_norm, mla_kv_norm=mla_kv_norm, mla_w_uq=mla_w_uq,
                  mla_w_ukv=mla_w_ukv, mla_w_o=mla_w_o, w_out=w_out, mlp_norm=mlp_norm, w_up=w_up,
                  w_down=w_down, final_norm=final_norm)
    return _forward(x, params, w_in.shape[0])
```
